```python
import math
import jax
import jax.numpy as jnp
from jax import lax
import numpy as np

D_MODEL = 1024
BATCH = 16
SEQ = 2048
DEPTH = 2
DEC_BATCH = 32
DEC_SEQ = 8
PAST_LEN = 16384
PAGE_SIZE = 128

N_MIXERS = 2
N_RWKV_LAYERS = (DEPTH + 1) // 2
N_MOBA_LAYERS = DEPTH // 2
HEAD_DIM = 64
N_HEADS = D_MODEL // HEAD_DIM
D_DECAY_LORA = 64
D_AAA_LORA = 64
D_GATE_LORA = 128
GN_EPS = 64e-5
MOBA_BLOCK = 256
MOBA_TOP_K = 3
Q_BLOCK = 128
N_BUCKETS = 32
MAX_DISTANCE = 128
D_FF = 4 * D_MODEL
RMS_EPS = 1e-6
NEG_BIG = -1e30

kernel_name = 'rwkv7_moba_hybrid_step'


def rmsnorm(x, g):
    xf = x.astype(jnp.float32)
    y = xf * lax.rsqrt(jnp.mean(xf * xf, axis=-1, keepdims=True) + RMS_EPS)
    return (y * g.astype(jnp.float32)).astype(x.dtype)


def sq_relu_mlp(h, w_up, w_down):
    return jnp.square(jax.nn.relu(h @ w_up)) @ w_down


def t5_bucket(rel):
    n = jnp.maximum(rel, 0)
    max_exact = N_BUCKETS // 2
    nf = jnp.maximum(n, 1).astype(jnp.float32)
    large = max_exact + (jnp.log(nf / max_exact) / math.log(MAX_DISTANCE / max_exact)
                         * (N_BUCKETS - max_exact)).astype(jnp.int32)
    large = jnp.minimum(large, N_BUCKETS - 1)
    return jnp.where(n < max_exact, n, large)


def rwkv7_time_mix(h, h_prev, s0, mix, w_r, w_k, w_v, w_o, w0, w1, w2, a0, a1, a2,
                   g1, g2, k_k, k_a, r_k, ln_w, ln_b):
    B, T, D = h.shape
    H, N = N_HEADS, HEAD_DIM
    f32 = jnp.float32
    dx = jnp.concatenate([h_prev[:, None, :], h[:, :-1, :]], axis=1) - h
    xr, xw, xk, xv, xa, xg = [h + dx * mix[i] for i in range(6)]
    r = (xr @ w_r).astype(f32)
    k = (xk @ w_k).astype(f32)
    v = (xv @ w_v).astype(f32)
    w = -jax.nn.softplus(-(w0 + jnp.tanh(xw @ w1) @ w2).astype(f32)) - 0.5
    decay = jnp.exp(-jnp.exp(w))
    a = jax.nn.sigmoid((a0 + (xa @ a1) @ a2).astype(f32))
    g = (jax.nn.sigmoid(xg @ g1) @ g2).astype(f32)
    kk = (k * k_k.astype(f32)).reshape(B, T, H, N)
    kk = kk * lax.rsqrt(jnp.maximum(jnp.sum(kk * kk, axis=-1, keepdims=True), 1e-24))
    k = k * (1.0 + (a - 1.0) * k_a.astype(f32))
    heads = lambda z: z.reshape(B, T, H, N)
    r, k, v, decay, a = heads(r), heads(k), heads(v), heads(decay), heads(a)

    def step(S, inp):
        r_t, w_t, k_t, v_t, kk_t, b_t = inp
        sa = jnp.einsum('bhvk,bhk->bhv', S, -kk_t)
        S = (S * w_t[:, :, None, :] + sa[..., None] * b_t[:, :, None, :]
             + v_t[..., None] * k_t[:, :, None, :])
        return S, jnp.einsum('bhvk,bhk->bhv', S, r_t)

    tm = lambda z: jnp.moveaxis(z, 1, 0)
    s_final, y = lax.scan(step, s0.astype(f32),
                          (tm(r), tm(decay), tm(k), tm(v), tm(kk), tm(kk * a)))
    y = jnp.moveaxis(y, 0, 1)
    mu = jnp.mean(y, axis=-1, keepdims=True)
    var = jnp.mean(jnp.square(y - mu), axis=-1, keepdims=True)
    y = ((y - mu) * lax.rsqrt(var + GN_EPS)).reshape(B, T, D) * ln_w.astype(f32) + ln_b.astype(f32)
    bonus = (jnp.sum(r * k * r_k.astype(f32), axis=-1, keepdims=True) * v).reshape(B, T, D)
    out = ((y + bonus) * g).astype(h.dtype) @ w_o
    return out, h[:, -1, :], s_final.astype(s0.dtype)


def _attend_two(qb, qpos, ks, vs, kpos, valid, ko, vo, opos, rel_bias):
    f32 = jnp.float32
    scale = HEAD_DIM ** -0.5
    hidx = jnp.arange(N_HEADS)
    L = ks.shape[2]
    s_sel = (jnp.einsum('thd,thld->thl', qb, ks).astype(f32) * scale
             + rel_bias[t5_bucket(qpos[:, None, None] - kpos), hidx[None, :, None]].astype(f32))
    s_sel = jnp.where(valid, s_sel, NEG_BIG)
    rel_o = qpos[:, None] - opos[None, :]
    s_own = (jnp.einsum('thd,lhd->thl', qb, ko).astype(f32) * scale
             + jnp.transpose(rel_bias[t5_bucket(rel_o)], (0, 2, 1)).astype(f32))
    s_own = jnp.where((rel_o >= 0)[:, None, :], s_own, NEG_BIG)
    p = jax.nn.softmax(jnp.concatenate([s_sel, s_own], axis=-1), axis=-1)
    return (jnp.einsum('thl,thld->thd', p[..., :L].astype(vs.dtype), vs)
            + jnp.einsum('thl,lhd->thd', p[..., L:].astype(vo.dtype), vo))


def moba_prompt(q, k, v, rel_bias):
    B, S, H, Dh = q.shape
    f32 = jnp.float32
    nblk = -(-S // MOBA_BLOCK)
    pad = nblk * MOBA_BLOCK - S
    padw = ((0, 0), (0, pad), (0, 0), (0, 0))
    kb = jnp.pad(k, padw).reshape(B, nblk, MOBA_BLOCK, H, Dh)
    vb = jnp.pad(v, padw).reshape(B, nblk, MOBA_BLOCK, H, Dh)
    kmean = jnp.mean(kb.astype(f32), axis=2)
    qpos_all = jnp.arange(S)
    gate = jnp.einsum('bshd,bnhd->bshn', q.astype(f32), kmean)
    full = jnp.arange(nblk)[None, :] < (qpos_all // MOBA_BLOCK)[:, None]
    gate = jnp.where(full[None, :, None, :], gate, NEG_BIG)
    ksel = min(MOBA_TOP_K, nblk)
    _, top = lax.top_k(gate, ksel)
    nqc = S // Q_BLOCK
    q_c = q.reshape(B * nqc, Q_BLOCK, H, Dh)
    top_c = top.reshape(B * nqc, Q_BLOCK, H, ksel)
    items = jnp.arange(B * nqc)
    hsel = jnp.arange(H)[None, :, None]

    def one(args):
        it, qc, tc = args
        b = it // nqc
        qp = (it % nqc) * Q_BLOCK + jnp.arange(Q_BLOCK)
        kb_b, vb_b = kb[b], vb[b]
        ks = kb_b[tc, :, hsel, :].reshape(Q_BLOCK, H, ksel * MOBA_BLOCK, Dh)
        vs = vb_b[tc, :, hsel, :].reshape(Q_BLOCK, H, ksel * MOBA_BLOCK, Dh)
        kpos = (tc[..., None] * MOBA_BLOCK + jnp.arange(MOBA_BLOCK)).reshape(Q_BLOCK, H, ksel * MOBA_BLOCK)
        valid = jnp.repeat(jnp.arange(ksel)[None, :] < (qp // MOBA_BLOCK)[:, None],
                           MOBA_BLOCK, axis=1)[:, None, :]
        ob = qp[0] // MOBA_BLOCK
        opos = ob * MOBA_BLOCK + jnp.arange(MOBA_BLOCK)
        return _attend_two(qc, qp, ks, vs, kpos, valid, kb_b[ob], vb_b[ob], opos, rel_bias)

    out = lax.map(one, (items, q_c, top_c))
    return out.reshape(B, S, H, Dh)


def moba_sample(q, k_new, v_new, cache_k, cache_v, layer, page_table, rel_bias):
    Bd, T, H, Dh = q.shape
    f32 = jnp.float32
    cb = PAST_LEN // MOBA_BLOCK
    own_past = PAST_LEN - cb * MOBA_BLOCK
    ppb = MOBA_BLOCK // PAGE_SIZE
    ksel = min(MOBA_TOP_K, cb)
    qpos = PAST_LEN + jnp.arange(T)
    opos = jnp.concatenate([cb * MOBA_BLOCK + jnp.arange(own_past), qpos])
    hsel = jnp.arange(H)[None, :, None, None]

    def one(args):
        qb, kn, vn, pt = args
        if ksel > 0:
            kpast = cache_k[layer, pt[:cb * ppb]].reshape(cb, MOBA_BLOCK, H, Dh)
            kmean = jnp.mean(kpast.astype(f32), axis=1)
            gate = jnp.einsum('thd,nhd->thn', qb.astype(f32), kmean)
            _, top = lax.top_k(gate, ksel)
            phys = pt[top[..., None] * ppb + jnp.arange(ppb)]
            ks = cache_k[layer, phys, :, hsel, :].reshape(T, H, ksel * MOBA_BLOCK, Dh)
            vs = cache_v[layer, phys, :, hsel, :].reshape(T, H, ksel * MOBA_BLOCK, Dh)
            kpos = (top[..., None] * MOBA_BLOCK + jnp.arange(MOBA_BLOCK)).reshape(T, H, ksel * MOBA_BLOCK)
        else:
            ks = jnp.zeros((T, H, 0, Dh), qb.dtype)
            vs = jnp.zeros((T, H, 0, Dh), qb.dtype)
            kpos = jnp.zeros((T, H, 0), jnp.int32)
        valid = jnp.ones((T, 1, kpos.shape[-1]), bool)
        own_pages = pt[cb * ppb:]
        ko = jnp.concatenate([cache_k[layer, own_pages].reshape(own_past, H, Dh).astype(kn.dtype), kn], axis=0)
        vo = jnp.concatenate([cache_v[layer, own_pages].reshape(own_past, H, Dh).astype(vn.dtype), vn], axis=0)
        return _attend_two(qb, qpos, ks, vs, kpos, valid, ko, vo, opos, rel_bias)

    return lax.map(one, (q, k_new, v_new, page_table))


def setup_inputs(seed: int = 0) -> dict:
    key = jax.random.key(seed)
    ks = iter(jax.random.split(key, 48))
    f32 = jnp.float32
    D, H, N = D_MODEL, N_HEADS, HEAD_DIM
    NR, NM = N_RWKV_LAYERS, N_MOBA_LAYERS
    n_pages = PAST_LEN // PAGE_SIZE
    n_pool = (DEC_BATCH * n_pages * 5) // 4
    nrm = lambda shape, s: jax.random.normal(next(ks), shape, f32) * s
    gain = lambda shape: 1.0 + nrm(shape, 0.02)
    perm = jax.random.permutation(next(ks), n_pool)
    page_table = perm[:DEC_BATCH * n_pages].reshape(DEC_BATCH, n_pages).astype(jnp.int32)
    return {
        'x_prompt': nrm((BATCH, SEQ, D), 1.0),
        'x_sample': nrm((DEC_BATCH, DEC_SEQ, D), 1.0),
        'state_wkv': nrm((NR, DEC_BATCH, H, N, N), 0.3),
        'state_shift': nrm((NR, DEC_BATCH, D), 1.0),
        'cache_k': nrm((NM, n_pool, PAGE_SIZE, H, N), 1.0),
        'cache_v': nrm((NM, n_pool, PAGE_SIZE, H, N), 1.0),
        'page_table': page_table,
        'ln1': gain((DEPTH, D)),
        'ln2': gain((DEPTH, D)),
        'ln_f': gain((D,)),
        'mlp_up': nrm((DEPTH, D, D_FF), D ** -0.5),
        'mlp_down': nrm((DEPTH, D_FF, D), D_FF ** -0.5),
        'rwkv_mix': jax.random.uniform(next(ks), (NR, 6, D), f32),
        'rwkv_w_r': nrm((NR, D, D), D ** -0.5),
        'rwkv_w_k': nrm((NR, D, D), D ** -0.5),
        'rwkv_w_v': nrm((NR, D, D), D ** -0.5),
        'rwkv_w_o': nrm((NR, D, D), D ** -0.5),
        'rwkv_w0': jax.random.uniform(next(ks), (NR, D), f32, -6.0, -1.0),
        'rwkv_w1': nrm((NR, D, D_DECAY_LORA), D ** -0.5),
        'rwkv_w2': nrm((NR, D_DECAY_LORA, D), 0.5 * D_DECAY_LORA ** -0.5),
        'rwkv_a0': nrm((NR, D), 0.3),
        'rwkv_a1': nrm((NR, D, D_AAA_LORA), D ** -0.5),
        'rwkv_a2': nrm((NR, D_AAA_LORA, D), D_AAA_LORA ** -0.5),
        'rwkv_g1': nrm((NR, D, D_GATE_LORA), D ** -0.5),
        'rwkv_g2': nrm((NR, D_GATE_LORA, D), D_GATE_LORA ** -0.5),
        'rwkv_k_k': 0.85 + nrm((NR, D), 0.05),
        'rwkv_k_a': 1.0 + nrm((NR, D), 0.05),
        'rwkv_r_k': nrm((NR, H, N), 0.1),
        'rwkv_ln_w': gain((NR, D)),
        'rwkv_ln_b': nrm((NR, D), 0.02),
        'moba_w_qkv': nrm((NM, D, 3 * D), D ** -0.5),
        'moba_w_o': nrm((NM, D, D), D ** -0.5),
        'rel_bias': nrm((N_BUCKETS, H), 0.5),
    }


def reference(x_prompt, x_sample, state_wkv, state_shift, cache_k, cache_v, page_table,
              ln1, ln2, ln_f, mlp_up, mlp_down,
              rwkv_mix, rwkv_w_r, rwkv_w_k, rwkv_w_v, rwkv_w_o, rwkv_w0, rwkv_w1, rwkv_w2,
              rwkv_a0, rwkv_a1, rwkv_a2, rwkv_g1, rwkv_g2, rwkv_k_k, rwkv_k_a, rwkv_r_k,
              rwkv_ln_w, rwkv_ln_b, moba_w_qkv, moba_w_o, rel_bias):
    B, S, D = x_prompt.shape
    Bd, T, _ = x_sample.shape
    H, Dh = N_HEADS, HEAD_DIM
    yp, ys = x_prompt, x_sample
    wkv_p, shift_p, k_p, v_p = [], [], [], []
    wkv_s, shift_s, k_s, v_s = [], [], [], []
    for i in range(DEPTH):
        j = i // N_MIXERS
        hp = rmsnorm(yp, ln1[i])
        hs = rmsnorm(ys, ln1[i])
        if i % N_MIXERS == 0:
            prm = (rwkv_mix[j], rwkv_w_r[j], rwkv_w_k[j], rwkv_w_v[j], rwkv_w_o[j], rwkv_w0[j],
                   rwkv_w1[j], rwkv_w2[j], rwkv_a0[j], rwkv_a1[j], rwkv_a2[j], rwkv_g1[j],
                   rwkv_g2[j], rwkv_k_k[j], rwkv_k_a[j], rwkv_r_k[j], rwkv_ln_w[j], rwkv_ln_b[j])
            op, lp, sp = rwkv7_time_mix(hp, jnp.zeros((B, D), hp.dtype),
                                        jnp.zeros((B, H, Dh, Dh), state_wkv.dtype), *prm)
            os_, ls, ss = rwkv7_time_mix(hs, state_shift[j], state_wkv[j], *prm)
            wkv_p.append(sp)
            shift_p.append(lp)
            wkv_s.append(ss)
            shift_s.append(ls)
        else:
            qkv_p = (hp @ moba_w_qkv[j]).reshape(B, S, 3, H, Dh)
            qp_, kp_, vp_ = qkv_p[:, :, 0], qkv_p[:, :, 1], qkv_p[:, :, 2]
            op = moba_prompt(qp_, kp_, vp_, rel_bias).reshape(B, S, D) @ moba_w_o[j]
            qkv_s = (hs @ moba_w_qkv[j]).reshape(Bd, T, 3, H, Dh)
            qs_, ks_, vs_ = qkv_s[:, :, 0], qkv_s[:, :, 1], qkv_s[:, :, 2]
            os_ = moba_sample(qs_, ks_, vs_, cache_k, cache_v, j, page_table,
                              rel_bias).reshape(Bd, T, D) @ moba_w_o[j]
            k_p.append(kp_)
            v_p.append(vp_)
            k_s.append(ks_)
            v_s.append(vs_)
        yp = yp + op
        ys = ys + os_
        yp = yp + sq_relu_mlp(rmsnorm(yp, ln2[i]), mlp_up[i], mlp_down[i])
        ys = ys + sq_relu_mlp(rmsnorm(ys, ln2[i]), mlp_up[i], mlp_down[i])
    y_prompt = rmsnorm(yp, ln_f)
    y_sample = rmsnorm(ys, ln_f)
    return (y_prompt, y_sample, jnp.stack(wkv_p), jnp.stack(shift_p), jnp.stack(k_p), jnp.stack(v_p),
            jnp.stack(wkv_s), jnp.stack(shift_s), jnp.stack(k_s), jnp.stack(v_s))
```

```python
import functools
import math

import numpy as np
import jax
import jax.numpy as jnp
from jax import lax
from jax.experimental import pallas as pl
from jax.experimental.pallas import tpu as pltpu

F32 = jnp.float32
BF16 = jnp.bfloat16

HEAD_DIM = 64
GN_EPS = 64e-5
RMS_EPS = 1e-6
NEG_BIG = -1e30
MOBA_BLOCK = 256
MOBA_TOP_K = 3
N_BUCKETS = 32
MAX_DISTANCE = 128
KK_EPS = 1e-24

V7X_LANES = 128
V7X_SUBLANES = 8
V7X_VMEM_LIMIT_BYTES = 52 * 1024 * 1024

HI = lax.Precision.HIGHEST


def _cparams(sem):
    return pltpu.CompilerParams(dimension_semantics=sem, vmem_limit_bytes=V7X_VMEM_LIMIT_BYTES)


def _const_spec(shape):
    nd = len(shape)
    return pl.BlockSpec(shape, lambda *_: (0,) * nd)


def _rms(x, g):
    return x * lax.rsqrt(jnp.mean(x * x, axis=-1, keepdims=True) + RMS_EPS) * g


def _bdot(a, b):
    return jnp.dot(a.astype(BF16), b.astype(BF16), preferred_element_type=F32)


def _dot_nt(a, b, precision=None):
    return lax.dot_general(a, b, (((1,), (1,)), ((), ())), precision=precision,
                           preferred_element_type=F32)


def _dot_tn(a, b, precision=None):
    return lax.dot_general(a, b, (((0,), (0,)), ((), ())), precision=precision,
                           preferred_element_type=F32)


def _rwkv_proj_body(x_ref, hprev_ref, ln_ref, mix_ref, wr_ref, wk_ref, wv_ref,
                    w1_ref, a1_ref, g1_ref, w2_ref, a2_ref, g2_ref, w0_ref, a0_ref,
                    r_ref, k_ref, v_ref, wl_ref, a_ref, g_ref, shift_ref, hbuf):
    j = pl.program_id(1)
    tm = x_ref.shape[0]
    pad = V7X_SUBLANES
    h = _rms(x_ref[...], ln_ref[...])

    @pl.when(j == 0)
    def _():
        hbuf[pad - 1:pad, :] = hprev_ref[...]

    hbuf[pad:pad + tm, :] = h
    dx = hbuf[pad - 1:pad - 1 + tm, :] - h
    hbuf[pad - 1:pad, :] = h[tm - 1:tm, :]
    shift_ref[...] = h[tm - 1:tm, :]

    def mixed(i):
        return (h + dx * mix_ref[i:i + 1, :]).astype(BF16)

    r_ref[...] = jnp.dot(mixed(0), wr_ref[...], preferred_element_type=F32)
    k_ref[...] = jnp.dot(mixed(2), wk_ref[...], preferred_element_type=F32)
    v_ref[...] = jnp.dot(mixed(3), wv_ref[...], preferred_element_type=F32)
    tw = jnp.tanh(jnp.dot(mixed(1), w1_ref[...], preferred_element_type=F32))
    wpre = w0_ref[...] + _bdot(tw, w2_ref[...])
    sp = jnp.maximum(-wpre, 0.0) + jnp.log(1.0 + jnp.exp(-jnp.abs(wpre)))
    wl_ref[...] = -jnp.exp(-sp - 0.5)
    ta = jnp.dot(mixed(4), a1_ref[...], preferred_element_type=F32)
    a_ref[...] = jax.nn.sigmoid(a0_ref[...] + _bdot(ta, a2_ref[...]))
    tg = jax.nn.sigmoid(jnp.dot(mixed(5), g1_ref[...], preferred_element_type=F32))
    g_ref[...] = _bdot(tg, g2_ref[...])


def _rwkv_proj(x, hprev, ln, mix, wr, wk, wv, w1, a1, g1, w2, a2, g2, w0, a0, tm):
    nb, t, d = x.shape
    assert t % tm == 0
    tok = pl.BlockSpec((None, tm, d), lambda b, j: (b, j, 0))
    row = pl.BlockSpec((None, 1, d), lambda b, j: (b, 0, 0))
    ins = [x, hprev.reshape(nb, 1, d), ln, mix, wr, wk, wv, w1, a1, g1, w2, a2, g2, w0, a0]
    in_specs = [tok, row] + [_const_spec(a.shape) for a in ins[2:]]
    big = jax.ShapeDtypeStruct((nb, t, d), F32)
    outs = pl.pallas_call(
        _rwkv_proj_body,
        grid=(nb, t // tm),
        in_specs=in_specs,
        out_specs=[tok] * 6 + [row],
        out_shape=[big] * 6 + [jax.ShapeDtypeStruct((nb, 1, d), F32)],
        scratch_shapes=[pltpu.VMEM((tm + 2 * V7X_SUBLANES, d), F32)],
        compiler_params=_cparams(("arbitrary", "arbitrary")),
        name="rwkv_proj",
    )(*ins)
    return outs[:6], outs[6].reshape(nb, d)


def _unit_lower_inverse(x, eye, n_square):
    p = eye + x
    xp = x
    for _ in range(n_square):
        xp = jnp.dot(xp, xp, precision=HI, preferred_element_type=F32)
        p = p + jnp.dot(p, xp, precision=HI, preferred_element_type=F32)
    return p


def _rwkv_head_chunk(r, k, v, wl, a, g, s0, kkp, kap, rkp, lnw, lnb, masks):
    tril, strict, eye, n_square = masks
    c = r.shape[0]
    kk = k * kkp
    kk = kk * lax.rsqrt(jnp.maximum(jnp.sum(kk * kk, axis=-1, keepdims=True), KK_EPS))
    k2 = k * (1.0 + (a - 1.0) * kap)
    b = kk * a
    cum = jnp.dot(tril, wl, precision=HI, preferred_element_type=F32)
    cum_end = cum[c - 1:c, :]
    kkt = kk * jnp.exp(cum - wl)
    rt = r * jnp.exp(cum)
    inv_w = jnp.exp(-cum)
    bt = b * inv_w
    kt = k2 * inv_w
    to_end = jnp.exp(cum_end - cum)
    bh = b * to_end
    kh = k2 * to_end
    zero = jnp.zeros((c, c), F32)
    mab = jnp.where(strict, _dot_nt(kkt, bt, HI), zero)
    mak = jnp.where(strict, _dot_nt(kkt, kt, HI), zero)
    nab = jnp.where(tril > 0, _dot_nt(rt, bt, HI), zero)
    nak = jnp.where(tril > 0, _dot_nt(rt, kt, HI), zero)
    inv = _unit_lower_inverse(-mab, eye, n_square)
    rhs = _dot_nt(kkt, s0, HI) + jnp.dot(mak, v, precision=HI, preferred_element_type=F32)
    sa = -jnp.dot(inv, rhs, precision=HI, preferred_element_type=F32)
    y = (_dot_nt(rt, s0, HI)
         + jnp.dot(nab, sa, precision=HI, preferred_element_type=F32)
         + jnp.dot(nak, v, precision=HI, preferred_element_type=F32))
    s_new = s0 * jnp.exp(cum_end) + _dot_tn(sa, bh, HI) + _dot_tn(v, kh, HI)
    mu = jnp.mean(y, axis=-1, keepdims=True)
    yc = y - mu
    var = jnp.mean(yc * yc, axis=-1, keepdims=True)
    yn = yc * lax.rsqrt(var + GN_EPS) * lnw + lnb
    bonus = jnp.sum(r * k2 * rkp, axis=-1, keepdims=True) * v
    return (yn + bonus) * g, s_new


def _rwkv_scan_body(r_ref, k_ref, v_ref, wl_ref, a_ref, g_ref, s0_ref,
                    kkp_ref, kap_ref, rkp_ref, lnw_ref, lnb_ref,
                    z_ref, sout_ref, s_scr):
    ci = pl.program_id(2)
    c = r_ref.shape[0]
    n = HEAD_DIM
    heads = r_ref.shape[1] // n

    @pl.when(ci == 0)
    def _():
        s_scr[...] = s0_ref[...]

    row = lax.broadcasted_iota(jnp.int32, (c, c), 0)
    col = lax.broadcasted_iota(jnp.int32, (c, c), 1)
    masks = ((col <= row).astype(F32), col < row, (col == row).astype(F32),
             max(int(math.ceil(math.log2(c))) - 1, 0))
    for hh in range(heads):
        sl = slice(hh * n, (hh + 1) * n)
        z, s_new = _rwkv_head_chunk(
            r_ref[:, sl], k_ref[:, sl], v_ref[:, sl], wl_ref[:, sl], a_ref[:, sl], g_ref[:, sl],
            s_scr[hh], kkp_ref[:, sl], kap_ref[:, sl], rkp_ref[:, sl], lnw_ref[:, sl],
            lnb_ref[:, sl], masks)
        z_ref[:, sl] = z
        s_scr[hh] = s_new

    @pl.when(ci == pl.num_programs(2) - 1)
    def _():
        sout_ref[...] = s_scr[...]


def _rwkv_scan(r, k, v, wl, a, g, s0, kkp, kap, rkp, lnw, lnb, chunk, heads_per_step):
    nb, t, d = r.shape
    n = HEAD_DIM
    nh = d // n
    gw = heads_per_step * n
    assert t % chunk == 0 and nh % heads_per_step == 0 and gw % V7X_LANES == 0
    tok = pl.BlockSpec((None, chunk, gw), lambda b, hg, ci: (b, ci, hg))
    st = pl.BlockSpec((None, heads_per_step, n, n), lambda b, hg, ci: (b, hg, 0, 0))
    par = pl.BlockSpec((1, gw), lambda b, hg, ci: (0, hg))
    z, s_out = pl.pallas_call(
        _rwkv_scan_body,
        grid=(nb, nh // heads_per_step, t // chunk),
        in_specs=[tok] * 6 + [st] + [par] * 5,
        out_specs=[tok, st],
        out_shape=[jax.ShapeDtypeStruct((nb, t, d), F32),
                   jax.ShapeDtypeStruct((nb, nh, n, n), F32)],
        scratch_shapes=[pltpu.VMEM((heads_per_step, n, n), F32)],
        compiler_params=_cparams(("arbitrary", "arbitrary", "arbitrary")),
        name="rwkv_scan",
    )(r, k, v, wl, a, g, s0, kkp, kap, rkp, lnw, lnb)
    return z, s_out


def _out_mlp_body(x_ref, z_ref, wo_ref, ln2_ref, up_ref, dn_ref, lnf_ref, y_ref, *, ff_chunk, final_norm):
    yp = x_ref[...] + jnp.dot(z_ref[...].astype(BF16), wo_ref[...], preferred_element_type=F32)
    h = _rms(yp, ln2_ref[...]).astype(BF16)
    acc = yp
    for c0 in range(0, up_ref.shape[1], ff_chunk):
        u = jnp.dot(h, up_ref[:, c0:c0 + ff_chunk], preferred_element_type=F32)
        u = jnp.square(jnp.maximum(u, 0.0)).astype(BF16)
        acc = acc + jnp.dot(u, dn_ref[c0:c0 + ff_chunk, :], preferred_element_type=F32)
    y_ref[...] = _rms(acc, lnf_ref[...]) if final_norm else acc


def _out_mlp(x, z, wo, ln2, up, dn, lnf, tm, final_norm):
    m, d = x.shape
    assert m % tm == 0
    tok = pl.BlockSpec((tm, d), lambda i: (i, 0))
    ins = [x, z, wo, ln2, up, dn, lnf]
    return pl.pallas_call(
        functools.partial(_out_mlp_body, ff_chunk=1024, final_norm=final_norm),
        grid=(m // tm,),
        in_specs=[tok, tok] + [_const_spec(a.shape) for a in ins[2:]],
        out_specs=tok,
        out_shape=jax.ShapeDtypeStruct((m, d), F32),
        compiler_params=_cparams(("arbitrary",)),
        name="out_mlp",
    )(*ins)


def _qkv_body(x_ref, ln_ref, w_ref, q_ref, k_ref, v_ref):
    d = x_ref.shape[1]
    h = _rms(x_ref[...], ln_ref[...]).astype(BF16)
    q_ref[...] = jnp.dot(h, w_ref[:, 0:d], preferred_element_type=F32)
    k_ref[...] = jnp.dot(h, w_ref[:, d:2 * d], preferred_element_type=F32)
    v_ref[...] = jnp.dot(h, w_ref[:, 2 * d:3 * d], preferred_element_type=F32)


def _qkv(x, ln, w, tm):
    m, d = x.shape
    assert m % tm == 0
    tok = pl.BlockSpec((tm, d), lambda i: (i, 0))
    sds = jax.ShapeDtypeStruct((m, d), F32)
    return pl.pallas_call(
        _qkv_body,
        grid=(m // tm,),
        in_specs=[tok, _const_spec(ln.shape), _const_spec(w.shape)],
        out_specs=[tok] * 3,
        out_shape=[sds] * 3,
        compiler_params=_cparams(("arbitrary",)),
        name="moba_qkv",
    )(x, ln, w)


def _bucket_starts():
    n = np.arange(MAX_DISTANCE, dtype=np.float32)
    max_exact = N_BUCKETS // 2
    nf = np.maximum(n, 1.0).astype(np.float32)
    large = max_exact + (np.log(nf / np.float32(max_exact)) / np.float32(math.log(MAX_DISTANCE / max_exact))
                         * np.float32(N_BUCKETS - max_exact)).astype(np.int32)
    bucket = np.where(n < max_exact, n.astype(np.int32), np.minimum(large, N_BUCKETS - 1))
    starts = [int(np.argmax(bucket >= b)) if (bucket >= b).any() else MAX_DISTANCE
              for b in range(N_BUCKETS)]
    starts[N_BUCKETS - 1] = min(starts[N_BUCKETS - 1], MAX_DISTANCE)
    return starts


_BUCKET_STARTS = _bucket_starts()


def _bias_of_distance(rel, rb_ref, head):
    bias = jnp.full(rel.shape, rb_ref[0, head], F32)
    for b in range(1, N_BUCKETS):
        bias = jnp.where(rel >= _BUCKET_STARTS[b], rb_ref[b, head], bias)
    return bias


def _moba_bias_body(rb_ref, o_ref):
    h = pl.program_id(0)
    blk = o_ref.shape[-1]
    kj = lax.broadcasted_iota(jnp.int32, (blk, blk), 0)
    qi = lax.broadcasted_iota(jnp.int32, (blk, blk), 1)
    rel = qi - kj
    own = _bias_of_distance(jnp.maximum(rel, 0), rb_ref, h)
    o_ref[0] = jnp.where(rel >= 0, own, NEG_BIG)
    o_ref[1] = _bias_of_distance(rel + blk, rb_ref, h)


def _moba_bias_tiles(rel_bias, blk):
    nh = rel_bias.shape[1]
    return pl.pallas_call(
        _moba_bias_body,
        grid=(nh,),
        in_specs=[pl.BlockSpec(memory_space=pltpu.SMEM)],
        out_specs=pl.BlockSpec((None, 2, blk, blk), lambda h: (h, 0, 0, 0)),
        out_shape=jax.ShapeDtypeStruct((nh, 2, blk, blk), F32),
        compiler_params=_cparams(("arbitrary",)),
        name="moba_bias",
    )(rel_bias)


def _moba_prompt_body(rb_ref, q_ref, k_ref, v_ref, bias_ref, o_ref,
                      kmean_scr, vt_scr, sel_scr, m_scr, l_scr, acc_scr):
    i = pl.program_id(2)
    hp = pl.program_id(1)
    blk = q_ref.shape[0]
    n = HEAD_DIM
    nblk = k_ref.shape[0] // blk
    heads = q_ref.shape[1] // n
    scale = HEAD_DIM ** -0.5

    @pl.when(i == 0)
    def _():
        for b in range(nblk):
            kb = k_ref[b * blk:(b + 1) * blk, :]
            kmean_scr[b:b + 1, :] = jnp.mean(kb, axis=0, keepdims=True)
            vt_scr[b] = v_ref[b * blk:(b + 1) * blk, :].T

    lane = lax.broadcasted_iota(jnp.int32, (1, heads * n), 1)
    q = q_ref[...]
    blk_id = lax.broadcasted_iota(jnp.int32, (nblk, blk), 0)

    def attend(hh, k_blk, vt_blk, bias, sel_row, first):
        s = _dot_nt(k_blk.astype(BF16), qm[hh]) + bias
        if sel_row is not None:
            s = jnp.where(sel_row > 0, s, NEG_BIG)
        m_blk = jnp.max(s, axis=0, keepdims=True)
        if first:
            m_new = m_blk
            p = jnp.exp(s - m_new)
            l_scr[hh] = jnp.sum(p, axis=0, keepdims=True)
            acc_scr[hh] = jnp.dot(vt_blk.astype(BF16), p.astype(BF16), preferred_element_type=F32)
        else:
            m_old = m_scr[hh]
            m_new = jnp.maximum(m_old, m_blk)
            alpha = jnp.exp(m_old - m_new)
            p = jnp.exp(s - m_new)
            l_scr[hh] = l_scr[hh] * alpha + jnp.sum(p, axis=0, keepdims=True)
            acc_scr[hh] = acc_scr[hh] * alpha + jnp.dot(vt_blk.astype(BF16), p.astype(BF16),
                                                        preferred_element_type=F32)
        m_scr[hh] = m_new

    qm = []
    for hh in range(heads):
        in_head = (lane >= hh * n) & (lane < (hh + 1) * n)
        qh = jnp.where(in_head, q, 0.0)
        qm.append((qh * scale).astype(BF16))
        gate = _dot_nt(kmean_scr[...], qh, HI)
        cnt = jnp.zeros((nblk, blk), jnp.int32)
        for mb in range(nblk - 1):
            gm = gate[mb:mb + 1, :]
            beats = (gm > gate) | ((gm == gate) & (mb < blk_id))
            cnt = cnt + jnp.where(beats, (mb < i).astype(jnp.int32), 0)
        sel = (blk_id < i) & (cnt < MOBA_TOP_K)
        sel_scr[hh] = jnp.where(sel, 1.0, 0.0)

    k_own = k_ref[pl.ds(pl.multiple_of(i * blk, blk), blk), :]
    for hh in range(heads):
        attend(hh, k_own, vt_scr[i, hh * n:(hh + 1) * n, :], bias_ref[hh, 0], None, True)

    @pl.when(i >= 1)
    def _():
        k_adj = k_ref[pl.ds(pl.multiple_of((i - 1) * blk, blk), blk), :]
        for hh in range(heads):
            attend(hh, k_adj, vt_scr[i - 1, hh * n:(hh + 1) * n, :], bias_ref[hh, 1],
                   sel_scr[hh, pl.ds(i - 1, 1), :], False)

    def far_block(b, carry):
        k_far = k_ref[pl.ds(pl.multiple_of(b * blk, blk), blk), :]
        for hh in range(heads):
            attend(hh, k_far, vt_scr[b, hh * n:(hh + 1) * n, :],
                   rb_ref[N_BUCKETS - 1, hp * heads + hh], sel_scr[hh, pl.ds(b, 1), :], False)
        return carry

    lax.fori_loop(0, jnp.maximum(i - 1, 0), far_block, 0)

    out_t = jnp.concatenate([acc_scr[hh] / l_scr[hh] for hh in range(heads)], axis=0)
    o_ref[...] = out_t.T


def _moba_prompt(q, k, v, bias_tiles, rel_bias):
    nb, s, d = q.shape
    blk = MOBA_BLOCK
    n = HEAD_DIM
    heads = V7X_LANES // n
    gw = heads * n
    nblk = s // blk
    assert s % blk == 0
    qspec = pl.BlockSpec((None, blk, gw), lambda b, hp, i: (b, i, hp))
    kvspec = pl.BlockSpec((None, s, gw), lambda b, hp, i: (b, 0, hp))
    bspec = pl.BlockSpec((heads, 2, blk, blk), lambda b, hp, i: (hp, 0, 0, 0))
    return pl.pallas_call(
        _moba_prompt_body,
        grid=(nb, d // gw, nblk),
        in_specs=[pl.BlockSpec(memory_space=pltpu.SMEM), qspec, kvspec, kvspec, bspec],
        out_specs=qspec,
        out_shape=jax.ShapeDtypeStruct((nb, s, d), F32),
        scratch_shapes=[
            pltpu.VMEM((nblk, gw), F32),
            pltpu.VMEM((nblk, gw, blk), F32),
            pltpu.VMEM((heads, nblk, blk), F32),
            pltpu.VMEM((heads, 1, blk), F32),
            pltpu.VMEM((heads, 1, blk), F32),
            pltpu.VMEM((heads, n, blk), F32),
        ],
        compiler_params=_cparams(("arbitrary", "arbitrary", "arbitrary")),
        name="moba_prompt",
    )(rel_bias, q, k, v, bias_tiles)


def _moba_gate_body(pt_ref, q_ref, k0_ref, k1_ref, top_ref, gate_scr):
    nb_ = pl.program_id(1)
    nblocks = pl.num_programs(1)
    rows = k0_ref.shape[0] + k1_ref.shape[0]
    kmean = (jnp.sum(k0_ref[...], axis=0) + jnp.sum(k1_ref[...], axis=0)) * (1.0 / rows)
    gate_scr[nb_] = jnp.sum(q_ref[...] * kmean[None], axis=-1)

    @pl.when(nb_ == nblocks - 1)
    def _():
        g = gate_scr[...]
        idx = lax.broadcasted_iota(jnp.int32, g.shape, 0)
        for j in range(MOBA_TOP_K):
            mx = jnp.max(g, axis=0, keepdims=True)
            first = jnp.min(jnp.where(g == mx, idx, nblocks), axis=0, keepdims=True)
            top_ref[j] = first[0]
            g = jnp.where(idx == first, -jnp.inf, g)


def _moba_gate_topk(q, cache_k, layer, page_table, pages_per_block, n_past_blocks):
    nb, t, nh, n = q.shape
    page = cache_k.shape[2]
    assert pages_per_block == 2

    def kspec(which):
        return pl.BlockSpec((None, None, page, nh, n),
                            lambda b, j, pt: (layer, pt[b, j * pages_per_block + which], 0, 0, 0))

    grid_spec = pltpu.PrefetchScalarGridSpec(
        num_scalar_prefetch=1,
        grid=(nb, n_past_blocks),
        in_specs=[pl.BlockSpec((None, t, nh, n), lambda b, j, pt: (b, 0, 0, 0)), kspec(0), kspec(1)],
        out_specs=pl.BlockSpec((None, MOBA_TOP_K, t, nh), lambda b, j, pt: (b, 0, 0, 0)),
        scratch_shapes=[pltpu.VMEM((n_past_blocks, t, nh), F32)],
    )
    return pl.pallas_call(
        _moba_gate_body,
        grid_spec=grid_spec,
        out_shape=jax.ShapeDtypeStruct((nb, MOBA_TOP_K, t, nh), jnp.int32),
        compiler_params=_cparams(("arbitrary", "arbitrary")),
        name="moba_gate",
    )(page_table, q, cache_k, cache_k)


def _moba_sample_body(top_ref, pt_ref, rb_ref, q_ref, kn_ref, vn_ref, ck_ref, cv_ref, o_ref,
                      kbuf, vbuf, sems, *, layer, n_pages, past_len, nh):
    b = pl.program_id(0)
    h = pl.program_id(1)
    t, n = q_ref.shape
    page = kbuf.shape[1]
    ppb = MOBA_BLOCK // page
    n_sel = MOBA_TOP_K * ppb
    scale = HEAD_DIM ** -0.5

    def block_of(ti, j):
        return top_ref[((b * MOBA_TOP_K + j) * t + ti) * nh + h]

    def copies(ti, j, p):
        phys = pt_ref[b * n_pages + block_of(ti, j) * ppb + p]
        slot = (ti * MOBA_TOP_K + j) * ppb + p
        return (pltpu.make_async_copy(ck_ref.at[layer, phys, :, h, :], kbuf.at[slot], sems.at[0]),
                pltpu.make_async_copy(cv_ref.at[layer, phys, :, h, :], vbuf.at[slot], sems.at[1]))

    every = [(ti, j, p) for ti in range(t) for j in range(MOBA_TOP_K) for p in range(ppb)]
    for idx in every:
        for cp in copies(*idx):
            cp.start()
    for idx in every:
        for cp in copies(*idx):
            cp.wait()

    q = q_ref[...]
    qs = (q * scale).astype(BF16)
    n_keys = t * n_sel * page
    seg = n_sel * page
    kall = kbuf[...].reshape(n_keys, n)
    vall = vbuf[...].reshape(n_keys, n)
    s = _dot_nt(qs, kall.astype(BF16))
    row = lax.broadcasted_iota(jnp.int32, (t, n_keys), 0)
    col = lax.broadcasted_iota(jnp.int32, (t, n_keys), 1)
    mine = (col >= row * seg) & (col < (row + 1) * seg)
    colr = lax.broadcasted_iota(jnp.int32, (1, n_keys), 1)
    kpos = jnp.zeros((1, n_keys), jnp.int32)
    for ti in range(t):
        for j in range(MOBA_TOP_K):
            start = (ti * MOBA_TOP_K + j) * MOBA_BLOCK
            inside = (colr >= start) & (colr < start + MOBA_BLOCK)
            kpos = jnp.where(inside, block_of(ti, j) * MOBA_BLOCK + colr - start, kpos)
    rel = (past_len + row) - kpos
    s = jnp.where(mine, s + _bias_of_distance(jnp.maximum(rel, 0), rb_ref, h), NEG_BIG)
    r_o = lax.broadcasted_iota(jnp.int32, (t, t), 0)
    c_o = lax.broadcasted_iota(jnp.int32, (t, t), 1)
    rel_o = r_o - c_o
    s_own = _dot_nt(qs, kn_ref[...].astype(BF16)) + _bias_of_distance(jnp.maximum(rel_o, 0), rb_ref, h)
    s_own = jnp.where(rel_o >= 0, s_own, NEG_BIG)
    m = jnp.maximum(jnp.max(s, axis=-1, keepdims=True), jnp.max(s_own, axis=-1, keepdims=True))
    p = jnp.exp(s - m)
    p_own = jnp.exp(s_own - m)
    denom = jnp.sum(p, axis=-1, keepdims=True) + jnp.sum(p_own, axis=-1, keepdims=True)
    o = (jnp.dot(p.astype(BF16), vall.astype(BF16), preferred_element_type=F32)
         + jnp.dot(p_own.astype(BF16), vn_ref[...].astype(BF16), preferred_element_type=F32))
    o_ref[...] = o / denom


def _moba_sample(q, k_new, v_new, top, page_table, rel_bias, cache_k, cache_v, layer, past_len):
    nb, nh, t, n = q.shape
    page = cache_k.shape[2]
    n_pages = page_table.shape[1]
    ppb = MOBA_BLOCK // page
    tok = pl.BlockSpec((None, None, t, n), lambda b, h, *_: (b, h, 0, 0))
    any_spec = pl.BlockSpec(memory_space=pl.ANY)
    grid_spec = pltpu.PrefetchScalarGridSpec(
        num_scalar_prefetch=2,
        grid=(nb, nh),
        in_specs=[pl.BlockSpec(memory_space=pltpu.SMEM), tok, tok, tok, any_spec, any_spec],
        out_specs=tok,
        scratch_shapes=[
            pltpu.VMEM((t * MOBA_TOP_K * ppb, page, n), F32),
            pltpu.VMEM((t * MOBA_TOP_K * ppb, page, n), F32),
            pltpu.SemaphoreType.DMA((2,)),
        ],
    )
    return pl.pallas_call(
        functools.partial(_moba_sample_body, layer=layer, n_pages=n_pages, past_len=past_len, nh=nh),
        grid_spec=grid_spec,
        out_shape=jax.ShapeDtypeStruct((nb, nh, t, n), F32),
        compiler_params=_cparams(("arbitrary", "arbitrary")),
        name="moba_sample",
    )(top.reshape(-1), page_table.reshape(-1), rel_bias, q, k_new, v_new, cache_k, cache_v)


def _tile(m, pref):
    return pref if m % pref == 0 else m


def kernel(x_prompt, x_sample, state_wkv, state_shift, cache_k, cache_v, page_table, ln1, ln2, ln_f, mlp_up, mlp_down, rwkv_mix, rwkv_w_r, rwkv_w_k, rwkv_w_v, rwkv_w_o, rwkv_w0, rwkv_w1, rwkv_w2, rwkv_a0, rwkv_a1, rwkv_a2, rwkv_g1, rwkv_g2, rwkv_k_k, rwkv_k_a, rwkv_r_k, rwkv_ln_w, rwkv_ln_b, moba_w_qkv, moba_w_o, rel_bias):
    nb, s, d = x_prompt.shape
    nbd, t, _ = x_sample.shape
    n = HEAD_DIM
    nh = d // n
    depth = ln1.shape[0]
    assert depth == 2 and rwkv_mix.shape[0] == 1 and moba_w_qkv.shape[0] == 1
    page = cache_k.shape[2]
    past_len = page_table.shape[1] * page
    assert past_len % MOBA_BLOCK == 0 and MOBA_BLOCK % page == 0
    n_past_blocks = past_len // MOBA_BLOCK
    assert n_past_blocks >= MOBA_TOP_K and s % MOBA_BLOCK == 0

    row = lambda a: a.reshape(1, -1)
    bf = lambda a: a.astype(BF16)
    lnf = row(ln_f)

    proj_w = (row(ln1[0]), rwkv_mix[0], bf(rwkv_w_r[0]), bf(rwkv_w_k[0]), bf(rwkv_w_v[0]),
              bf(rwkv_w1[0]), bf(rwkv_a1[0]), bf(rwkv_g1[0]), bf(rwkv_w2[0]), bf(rwkv_a2[0]),
              bf(rwkv_g2[0]), row(rwkv_w0[0]), row(rwkv_a0[0]))
    scan_w = (row(rwkv_k_k[0]), row(rwkv_k_a[0]), row(rwkv_r_k[0]), row(rwkv_ln_w[0]), row(rwkv_ln_b[0]))
    mlp0 = (bf(rwkv_w_o[0]), row(ln2[0]), bf(mlp_up[0]), bf(mlp_down[0]), lnf)
    mlp1 = (bf(moba_w_o[0]), row(ln2[1]), bf(mlp_up[1]), bf(mlp_down[1]), lnf)

    def rwkv_layer(x, hprev, s0, tm, chunk, tm_mlp):
        streams, shift = _rwkv_proj(x, hprev, *proj_w, tm=tm)
        z, s_out = _rwkv_scan(*streams, s0, *scan_w, chunk=chunk, heads_per_step=V7X_LANES // n)
        y = _out_mlp(x.reshape(-1, d), z.reshape(-1, d), *mlp0, tm=tm_mlp, final_norm=False)
        return y, shift, s_out

    yp, shift_p, wkv_p = rwkv_layer(x_prompt, jnp.zeros((nb, d), F32), jnp.zeros((nb, nh, n, n), F32),
                                    _tile(s, 256), _tile(s, 64), _tile(nb * s, 512))
    ys, shift_s, wkv_s = rwkv_layer(x_sample, state_shift[0], state_wkv[0], t, t, nbd * t)

    w_qkv = bf(moba_w_qkv[0])
    qp, kp, vp = _qkv(yp, row(ln1[1]), w_qkv, _tile(nb * s, 512))
    qs, ks, vs = _qkv(ys, row(ln1[1]), w_qkv, nbd * t)
    bias_tiles = _moba_bias_tiles(rel_bias, MOBA_BLOCK)
    attn_p = _moba_prompt(qp.reshape(nb, s, d), kp.reshape(nb, s, d), vp.reshape(nb, s, d),
                          bias_tiles, rel_bias)
    heads_first = lambda a: a.reshape(nbd, t, nh, n).transpose(0, 2, 1, 3)
    top = _moba_gate_topk(qs.reshape(nbd, t, nh, n), cache_k, 0, page_table,
                          MOBA_BLOCK // page, n_past_blocks)
    attn_s = _moba_sample(heads_first(qs), heads_first(ks), heads_first(vs), top, page_table,
                          rel_bias, cache_k, cache_v, 0, past_len)
    attn_s = attn_s.transpose(0, 2, 1, 3).reshape(nbd * t, d)
    y_prompt = _out_mlp(yp, attn_p.reshape(-1, d), *mlp1, tm=_tile(nb * s, 512), final_norm=True)
    y_sample = _out_mlp(ys, attn_s, *mlp1, tm=nbd * t, final_norm=True)

    return (y_prompt.reshape(nb, s, d), y_sample.reshape(nbd, t, d),
            wkv_p[None], shift_p[None],
            kp.reshape(1, nb, s, nh, n), vp.reshape(1, nb, s, nh, n),
            wkv_s[None], shift_s[None],
            ks.reshape(1, nbd, t, nh, n), vs.reshape(1, nbd, t, nh, n))
```

```python
import functools
import math

import numpy as np
import jax
import jax.numpy as jnp
from jax import lax
from jax.experimental import pallas as pl
from jax.experimental.pallas import tpu as pltpu

F32 = jnp.float32
BF16 = jnp.bfloat16

HEAD_DIM = 64
GN_EPS = 64e-5
RMS_EPS = 1e-6
NEG_BIG = -1e30
MOBA_BLOCK = 256
MOBA_TOP_K = 3
N_BUCKETS = 32
MAX_DISTANCE = 128
KK_EPS = 1e-24

V7X_LANES = 128
V7X_SUBLANES = 8
V7X_VMEM_LIMIT_BYTES = 52 * 1024 * 1024

HI = lax.Precision.HIGHEST
_NN = (((1,), (0,)), ((), ()))
_NT = (((1,), (1,)), ((), ()))
_TN = (((0,), (0,)), ((), ()))


def _cparams(sem):
    return pltpu.CompilerParams(dimension_semantics=sem, vmem_limit_bytes=V7X_VMEM_LIMIT_BYTES)


def _const_spec(shape):
    nd = len(shape)
    return pl.BlockSpec(shape, lambda *_: (0,) * nd)


def _rms(x, g):
    return x * lax.rsqrt(jnp.mean(x * x, axis=-1, keepdims=True) + RMS_EPS) * g


def _mm(a, b, dims=_NN):
    return lax.dot_general(a.astype(BF16), b.astype(BF16), dims, preferred_element_type=F32)


def _rwkv_proj_body(x_ref, hprev_ref, ln_ref, mix_ref, wr_ref, wk_ref, wv_ref,
                    w1_ref, a1_ref, g1_ref, w2_ref, a2_ref, g2_ref, w0_ref, a0_ref,
                    r_ref, k_ref, v_ref, wl_ref, a_ref, g_ref, shift_ref, hbuf):
    j = pl.program_id(1)
    tm = x_ref.shape[0]
    pad = V7X_SUBLANES
    h = _rms(x_ref[...], ln_ref[...])

    @pl.when(j == 0)
    def _():
        hbuf[pad - 1:pad, :] = hprev_ref[...]

    hbuf[pad:pad + tm, :] = h
    dx = hbuf[pad - 1:pad - 1 + tm, :] - h
    hbuf[pad - 1:pad, :] = h[tm - 1:tm, :]
    shift_ref[...] = h[tm - 1:tm, :]

    def mixed(i):
        return (h + dx * mix_ref[i:i + 1, :]).astype(BF16)

    r_ref[...] = jnp.dot(mixed(0), wr_ref[...], preferred_element_type=F32)
    k_ref[...] = jnp.dot(mixed(2), wk_ref[...], preferred_element_type=F32)
    v_ref[...] = jnp.dot(mixed(3), wv_ref[...], preferred_element_type=F32)
    tw = jnp.tanh(jnp.dot(mixed(1), w1_ref[...], preferred_element_type=F32))
    wpre = w0_ref[...] + _mm(tw, w2_ref[...])
    sp = jnp.maximum(-wpre, 0.0) + jnp.log(1.0 + jnp.exp(-jnp.abs(wpre)))
    wl_ref[...] = -jnp.exp(-sp - 0.5)
    ta = jnp.dot(mixed(4), a1_ref[...], preferred_element_type=F32)
    a_ref[...] = jax.nn.sigmoid(a0_ref[...] + _mm(ta, a2_ref[...]))
    tg = jax.nn.sigmoid(jnp.dot(mixed(5), g1_ref[...], preferred_element_type=F32))
    g_ref[...] = _mm(tg, g2_ref[...])


def _rwkv_proj(x, hprev, ln, mix, wr, wk, wv, w1, a1, g1, w2, a2, g2, w0, a0, tm):
    nb, t, d = x.shape
    assert t % tm == 0
    tok = pl.BlockSpec((None, tm, d), lambda b, j: (b, j, 0))
    row = pl.BlockSpec((None, 1, d), lambda b, j: (b, 0, 0))
    ins = [x, hprev.reshape(nb, 1, d), ln, mix, wr, wk, wv, w1, a1, g1, w2, a2, g2, w0, a0]
    in_specs = [tok, row] + [_const_spec(a.shape) for a in ins[2:]]
    big = jax.ShapeDtypeStruct((nb, t, d), F32)
    outs = pl.pallas_call(
        _rwkv_proj_body,
        grid=(nb, t // tm),
        in_specs=in_specs,
        out_specs=[tok] * 6 + [row],
        out_shape=[big] * 6 + [jax.ShapeDtypeStruct((nb, 1, d), F32)],
        scratch_shapes=[pltpu.VMEM((tm + 2 * V7X_SUBLANES, d), F32)],
        compiler_params=_cparams(("arbitrary", "arbitrary")),
        name="rwkv_proj",
    )(*ins)
    return outs[:6], outs[6].reshape(nb, d)


_INV_BASE = 8


def _rwkv_pairs_chunk(r, k, v, wl, a, g, sbd, kkp, kap, rkp, lnw, lnb, m):
    c = r[0].shape[0]
    ch0, th0, strict, incl, eye_cat, tril_bf, bd128, n_square, rowb, colb, levels = m
    zero = jnp.zeros((), F32)
    each = lambda fn, *ls: [fn(*xs) for xs in zip(*ls)]

    def seg_sum(x):
        s0 = jnp.sum(jnp.where(ch0, x, zero), axis=-1, keepdims=True)
        s1 = jnp.sum(jnp.where(ch0, zero, x), axis=-1, keepdims=True)
        return jnp.where(ch0, s0, s1)

    def bd_chan(x):
        return jnp.concatenate([jnp.where(ch0, x, zero), jnp.where(ch0, zero, x)], axis=0)

    def bd_chan_swapped(x):
        return jnp.concatenate([jnp.where(ch0, zero, x), jnp.where(ch0, x, zero)], axis=0)

    def bd_time(x):
        return jnp.concatenate([jnp.where(th0, x, zero), jnp.where(th0, zero, x)], axis=0)

    def normalised(k_, kkp_):
        kk_ = k_ * kkp_
        return kk_ * lax.rsqrt(jnp.maximum(seg_sum(kk_ * kk_), KK_EPS))

    def cumulative(wl_):
        w_hi = wl_.astype(BF16)
        w_r1 = wl_ - w_hi.astype(F32)
        w_mid = w_r1.astype(BF16)
        w_lo = (w_r1 - w_mid.astype(F32)).astype(BF16)
        return (jnp.dot(tril_bf, w_hi, preferred_element_type=F32)
                + (jnp.dot(tril_bf, w_mid, preferred_element_type=F32)
                   + jnp.dot(tril_bf, w_lo, preferred_element_type=F32)))

    kk = each(normalised, k, kkp)
    k2 = each(lambda k_, a_, kap_: k_ * (1.0 + (a_ - 1.0) * kap_), k, a, kap)
    b = each(lambda kk_, a_: kk_ * a_, kk, a)
    cum = each(cumulative, wl)
    cum_end = [x[c - 1:c, :] for x in cum]
    lhs = each(lambda kk_, r_, cum_, wl_: jnp.concatenate(
        [kk_ * jnp.exp(cum_ - wl_), r_ * jnp.exp(cum_)], axis=0), kk, r, cum, wl)
    inv_w = [jnp.exp(-x) for x in cum]
    bt = each(lambda b_, w_: b_ * w_, b, inv_w)
    kt = each(lambda k_, w_: k_ * w_, k2, inv_w)
    to_end = each(lambda e_, cum_: jnp.exp(e_ - cum_), cum_end, cum)
    rhs_end = each(lambda b_, k_, w_: jnp.concatenate([b_ * w_, k_ * w_], axis=0), b, k2, to_end)

    g0 = each(lambda l_, bt_, kt_: _mm(jnp.where(ch0, l_, zero), jnp.concatenate([bt_, kt_], axis=0), _NT),
              lhs, bt, kt)
    g1 = each(lambda l_, bt_, kt_: _mm(jnp.where(ch0, zero, l_), jnp.concatenate([kt_, bt_], axis=0), _NT),
              lhs, bt, kt)
    mab_c = each(lambda x, y: jnp.where(strict, jnp.where(th0, x[:c], y[:c]), zero), g0, g1)
    mak_s = each(lambda x, y: jnp.where(strict, jnp.where(th0, y[:c], x[:c]), zero), g0, g1)
    nab_c = each(lambda x, y: jnp.where(incl, jnp.where(th0, x[c:], y[c:]), zero), g0, g1)
    nak_s = each(lambda x, y: jnp.where(incl, jnp.where(th0, y[c:], x[c:]), zero), g0, g1)

    ls = each(lambda l_, s_: _mm(l_, s_, _NT), lhs, sbd)
    mn = each(lambda m_, n_, v_: _mm(jnp.concatenate([m_, n_], axis=0), bd_chan_swapped(v_)),
              mak_s, nak_s, v)

    xp = [jnp.where(rowb == colb, -x, zero) for x in mab_c]
    p = [eye_cat + x for x in xp]
    for _ in range(n_square):
        xp = each(lambda x: _mm(x, bd_time(x)), xp)
        p = each(lambda p_, x: p_ + _mm(p_, bd_time(x)), p, xp)
    for lev in levels:
        half = each(lambda p_, m_: _mm(p_, bd_time(jnp.where(lev, m_, zero))), p, mab_c)
        p = each(lambda p_, h_: p_ - _mm(h_, bd_time(p_)), p, half)

    sa = each(lambda p_, ls_, mn_: -_mm(p_, bd_chan(ls_[:c] + mn_[:c])), p, ls, mn)
    y = each(lambda ls_, n_, sa_, mn_: ls_[c:] + _mm(n_, bd_chan(sa_)) + mn_[c:], ls, nab_c, sa, mn)
    upd = each(lambda sa_, v_, e_: _mm(jnp.concatenate([sa_, v_], axis=0), e_, _TN), sa, v, rhs_end)
    sbd_new = each(lambda s_, e_, u_: s_ * jnp.exp(e_) + jnp.where(bd128, u_, zero), sbd, cum_end, upd)

    inv_n = 1.0 / HEAD_DIM

    def finish(y_, r_, k2_, v_, g_, rkp_, lnw_, lnb_):
        yc = y_ - seg_sum(y_) * inv_n
        var = seg_sum(yc * yc) * inv_n
        yn = yc * lax.rsqrt(var + GN_EPS) * lnw_ + lnb_
        return (yn + seg_sum(r_ * k2_ * rkp_) * v_) * g_

    return each(finish, y, r, k2, v, g, rkp, lnw, lnb), sbd_new


def _rwkv_scan_body(r_ref, k_ref, v_ref, wl_ref, a_ref, g_ref, s0_ref,
                    kkp_ref, kap_ref, rkp_ref, lnw_ref, lnb_ref,
                    z_ref, sout_ref, s_scr):
    ci = pl.program_id(2)
    c = r_ref.shape[0]
    n = HEAD_DIM
    w = 2 * n
    pairs = r_ref.shape[1] // w
    zpad = jnp.zeros((n, n), F32)

    @pl.when(ci == 0)
    def _():
        for pi in range(pairs):
            top = jnp.concatenate([s0_ref[2 * pi], zpad], axis=1)
            bot = jnp.concatenate([zpad, s0_ref[2 * pi + 1]], axis=1)
            s_scr[pi] = jnp.concatenate([top, bot], axis=0)

    lane = lax.broadcasted_iota(jnp.int32, (1, w), 1)
    rowt = lax.broadcasted_iota(jnp.int32, (c, 2 * c), 0)
    colt = lax.broadcasted_iota(jnp.int32, (c, 2 * c), 1)
    th0 = colt < c
    tcol = jnp.where(th0, colt, colt - c)
    row_c = lax.broadcasted_iota(jnp.int32, (c, c), 0)
    col_c = lax.broadcasted_iota(jnp.int32, (c, c), 1)
    r128 = lax.broadcasted_iota(jnp.int32, (w, w), 0)
    c128 = lax.broadcasted_iota(jnp.int32, (w, w), 1)
    base = min(c, _INV_BASE)
    base_log = int(math.log2(base))
    assert 1 << base_log == base and c % base == 0 and (c // base) & (c // base - 1) == 0
    levels = []
    size_log = base_log
    while (1 << size_log) < c:
        rb = lax.shift_right_logical(rowt, size_log)
        cb = lax.shift_right_logical(tcol, size_log)
        levels.append(((rb & 1) == 1) & (cb == rb - 1))
        size_log += 1
    masks = (lane < n, th0, tcol < rowt, tcol <= rowt, jnp.where(tcol == rowt, 1.0, 0.0).astype(F32),
             jnp.where(col_c <= row_c, 1.0, 0.0).astype(BF16), (r128 < n) == (c128 < n),
             base_log - 1, lax.shift_right_logical(rowt, base_log),
             lax.shift_right_logical(tcol, base_log), levels)
    lanes = [slice(pi * w, (pi + 1) * w) for pi in range(pairs)]
    per_pair = lambda ref: [ref[:, sl] for sl in lanes]
    z, s_new = _rwkv_pairs_chunk(
        per_pair(r_ref), per_pair(k_ref), per_pair(v_ref), per_pair(wl_ref), per_pair(a_ref),
        per_pair(g_ref), [s_scr[pi] for pi in range(pairs)], per_pair(kkp_ref), per_pair(kap_ref),
        per_pair(rkp_ref), per_pair(lnw_ref), per_pair(lnb_ref), masks)
    for pi in range(pairs):
        z_ref[:, lanes[pi]] = z[pi]
        s_scr[pi] = s_new[pi]

    @pl.when(ci == pl.num_programs(2) - 1)
    def _():
        for pi in range(pairs):
            s = s_scr[pi]
            sout_ref[2 * pi] = s[:n, :n]
            sout_ref[2 * pi + 1] = s[n:, n:]


def _rwkv_scan(r, k, v, wl, a, g, s0, kkp, kap, rkp, lnw, lnb, chunk, heads_per_step):
    nb, t, d = r.shape
    n = HEAD_DIM
    nh = d // n
    gw = heads_per_step * n
    assert t % chunk == 0 and nh % heads_per_step == 0 and gw % V7X_LANES == 0
    tok = pl.BlockSpec((None, chunk, gw), lambda b, hg, ci: (b, ci, hg))
    st = pl.BlockSpec((None, heads_per_step, n, n), lambda b, hg, ci: (b, hg, 0, 0))
    par = pl.BlockSpec((1, gw), lambda b, hg, ci: (0, hg))
    z, s_out = pl.pallas_call(
        _rwkv_scan_body,
        grid=(nb, nh // heads_per_step, t // chunk),
        in_specs=[tok] * 6 + [st] + [par] * 5,
        out_specs=[tok, st],
        out_shape=[jax.ShapeDtypeStruct((nb, t, d), F32),
                   jax.ShapeDtypeStruct((nb, nh, n, n), F32)],
        scratch_shapes=[pltpu.VMEM((heads_per_step // 2, 2 * n, 2 * n), F32)],
        compiler_params=_cparams(("arbitrary", "arbitrary", "arbitrary")),
        name="rwkv_scan",
    )(r, k, v, wl, a, g, s0, kkp, kap, rkp, lnw, lnb)
    return z, s_out


def _out_mlp_body(x_ref, z_ref, wo_ref, ln2_ref, up_ref, dn_ref, lnf_ref, y_ref, *, ff_chunk, final_norm):
    yp = x_ref[...] + jnp.dot(z_ref[...].astype(BF16), wo_ref[...], preferred_element_type=F32)
    h = _rms(yp, ln2_ref[...]).astype(BF16)
    acc = yp
    for c0 in range(0, up_ref.shape[1], ff_chunk):
        u = jnp.dot(h, up_ref[:, c0:c0 + ff_chunk], preferred_element_type=F32)
        u = jnp.square(jnp.maximum(u, 0.0)).astype(BF16)
        acc = acc + jnp.dot(u, dn_ref[c0:c0 + ff_chunk, :], preferred_element_type=F32)
    y_ref[...] = _rms(acc, lnf_ref[...]) if final_norm else acc


def _out_mlp(x, z, wo, ln2, up, dn, lnf, tm, final_norm):
    m, d = x.shape
    assert m % tm == 0
    tok = pl.BlockSpec((tm, d), lambda i: (i, 0))
    ins = [x, z, wo, ln2, up, dn, lnf]
    return pl.pallas_call(
        functools.partial(_out_mlp_body, ff_chunk=1024, final_norm=final_norm),
        grid=(m // tm,),
        in_specs=[tok, tok] + [_const_spec(a.shape) for a in ins[2:]],
        out_specs=tok,
        out_shape=jax.ShapeDtypeStruct((m, d), F32),
        compiler_params=_cparams(("arbitrary",)),
        name="out_mlp",
    )(*ins)


def _qkv_body(x_ref, ln_ref, w_ref, q_ref, k_ref, v_ref, *t_refs):
    d = x_ref.shape[1]
    h = _rms(x_ref[...], ln_ref[...]).astype(BF16)
    q_ref[...] = jnp.dot(h, w_ref[:, 0:d], preferred_element_type=F32)
    k = jnp.dot(h, w_ref[:, d:2 * d], preferred_element_type=F32)
    v = jnp.dot(h, w_ref[:, 2 * d:3 * d], preferred_element_type=F32)
    k_ref[...] = k
    v_ref[...] = v
    if t_refs:
        kt_ref, vt_ref = t_refs
        kt_ref[...] = k.T
        vt_ref[...] = v.T


def _qkv(x, ln, w, tm, transposed):
    nb, s, d = x.shape
    assert s % tm == 0
    tok = pl.BlockSpec((None, tm, d), lambda b, j: (b, j, 0))
    tok_t = pl.BlockSpec((None, d, tm), lambda b, j: (b, 0, j))
    sds = jax.ShapeDtypeStruct((nb, s, d), F32)
    sds_t = jax.ShapeDtypeStruct((nb, d, s), F32)
    return pl.pallas_call(
        _qkv_body,
        grid=(nb, s // tm),
        in_specs=[tok, _const_spec(ln.shape), _const_spec(w.shape)],
        out_specs=[tok] * 3 + ([tok_t] * 2 if transposed else []),
        out_shape=[sds] * 3 + ([sds_t] * 2 if transposed else []),
        compiler_params=_cparams(("arbitrary", "arbitrary")),
        name="moba_qkv",
    )(x, ln, w)


def _bucket_starts():
    n = np.arange(MAX_DISTANCE, dtype=np.float32)
    max_exact = N_BUCKETS // 2
    nf = np.maximum(n, 1.0).astype(np.float32)
    large = max_exact + (np.log(nf / np.float32(max_exact)) / np.float32(math.log(MAX_DISTANCE / max_exact))
                         * np.float32(N_BUCKETS - max_exact)).astype(np.int32)
    bucket = np.where(n < max_exact, n.astype(np.int32), np.minimum(large, N_BUCKETS - 1))
    starts = [int(np.argmax(bucket >= b)) if (bucket >= b).any() else MAX_DISTANCE
              for b in range(N_BUCKETS)]
    starts[N_BUCKETS - 1] = min(starts[N_BUCKETS - 1], MAX_DISTANCE)
    return starts


_BUCKET_STARTS = _bucket_starts()


def _bias_of_distance(rel, rb_ref, head):
    bias = jnp.full(rel.shape, rb_ref[0, head], F32)
    for b in range(1, N_BUCKETS):
        bias = jnp.where(rel >= _BUCKET_STARTS[b], rb_ref[b, head], bias)
    return bias


def _moba_bias_body(rb_ref, o_ref):
    h = pl.program_id(0)
    blk = o_ref.shape[-1]
    kj = lax.broadcasted_iota(jnp.int32, (blk, blk), 0)
    qi = lax.broadcasted_iota(jnp.int32, (blk, blk), 1)
    rel = qi - kj
    own = _bias_of_distance(jnp.maximum(rel, 0), rb_ref, h)
    o_ref[0] = jnp.where(rel >= 0, own, NEG_BIG)
    o_ref[1] = _bias_of_distance(rel + blk, rb_ref, h)


def _moba_bias_tiles(rel_bias, blk):
    nh = rel_bias.shape[1]
    return pl.pallas_call(
        _moba_bias_body,
        grid=(nh,),
        in_specs=[pl.BlockSpec(memory_space=pltpu.SMEM)],
        out_specs=pl.BlockSpec((None, 2, blk, blk), lambda h: (h, 0, 0, 0)),
        out_shape=jax.ShapeDtypeStruct((nh, 2, blk, blk), F32),
        compiler_params=_cparams(("arbitrary",)),
        name="moba_bias",
    )(rel_bias)


def _moba_prompt_body(rb_ref, q_ref, k_ref, v_ref, bias_ref, o_ref,
                      kmean_scr, vt_scr, sel_scr, m_scr, l_scr, acc_scr):
    i = pl.program_id(2)
    hp = pl.program_id(1)
    blk = q_ref.shape[0]
    n = HEAD_DIM
    nblk = k_ref.shape[0] // blk
    heads = q_ref.shape[1] // n
    scale = HEAD_DIM ** -0.5

    @pl.when(i == 0)
    def _():
        for b in range(nblk):
            kb = k_ref[b * blk:(b + 1) * blk, :]
            kmean_scr[b:b + 1, :] = jnp.mean(kb, axis=0, keepdims=True)
            vt_scr[b] = v_ref[b * blk:(b + 1) * blk, :].T

    lane = lax.broadcasted_iota(jnp.int32, (1, heads * n), 1)
    q = q_ref[...]
    blk_id = lax.broadcasted_iota(jnp.int32, (nblk, blk), 0)

    hs = range(heads)

    def attend(k_blk, vt_blk, bias, sel_row, first):
        kb = k_blk.astype(BF16)
        if sel_row is not None and bias[0].ndim == 0:
            bias = [jnp.where(sel_row[hh] > 0, bias[hh], NEG_BIG) for hh in hs]
            sel_row = None
        s = [lax.dot_general(kb, qm[hh], _NT, preferred_element_type=F32) + bias[hh] for hh in hs]
        if sel_row is not None:
            s = [jnp.where(sel_row[hh] > 0, s[hh], NEG_BIG) for hh in hs]
        m_new = [jnp.max(s[hh], axis=0, keepdims=True) for hh in hs]
        if not first:
            m_old = [m_scr[hh] for hh in hs]
            m_new = [jnp.maximum(m_old[hh], m_new[hh]) for hh in hs]
        p = [jnp.exp(s[hh] - m_new[hh]) for hh in hs]
        pv = [_mm(vt_blk[hh * n:(hh + 1) * n, :], p[hh]) for hh in hs]
        for hh in hs:
            l_blk = jnp.sum(p[hh], axis=0, keepdims=True)
            if first:
                l_scr[hh] = l_blk
                acc_scr[hh] = pv[hh]
            else:
                alpha = jnp.exp(m_old[hh] - m_new[hh])
                l_scr[hh] = l_scr[hh] * alpha + l_blk
                acc_scr[hh] = acc_scr[hh] * alpha + pv[hh]
            m_scr[hh] = m_new[hh]

    qm = []
    for hh in range(heads):
        in_head = (lane >= hh * n) & (lane < (hh + 1) * n)
        qh = jnp.where(in_head, q, 0.0)
        qm.append((qh * scale).astype(BF16))
        gate = lax.dot_general(kmean_scr[...], qh, _NT, precision=HI, preferred_element_type=F32)
        cnt = jnp.zeros((nblk, blk), jnp.int32)
        for mb in range(nblk - 1):
            gm = gate[mb:mb + 1, :]
            beats = (gm > gate) | ((gm == gate) & (mb < blk_id))
            cnt = cnt + jnp.where(beats, (mb < i).astype(jnp.int32), 0)
        sel = (blk_id < i) & (cnt < MOBA_TOP_K)
        sel_scr[hh] = jnp.where(sel, 1.0, 0.0)

    def k_block(b):
        return k_ref[pl.ds(pl.multiple_of(b * blk, blk), blk), :]

    attend(k_block(i), vt_scr[i], [bias_ref[hh, 0] for hh in hs], None, True)

    @pl.when(i >= 1)
    def _():
        attend(k_block(i - 1), vt_scr[i - 1], [bias_ref[hh, 1] for hh in hs],
               [sel_scr[hh, pl.ds(i - 1, 1), :] for hh in hs], False)

    def far_block(b, carry):
        attend(k_block(b), vt_scr[b], [rb_ref[N_BUCKETS - 1, hp * heads + hh] for hh in hs],
               [sel_scr[hh, pl.ds(b, 1), :] for hh in hs], False)
        return carry

    lax.fori_loop(0, jnp.maximum(i - 1, 0), far_block, 0)

    out_t = jnp.concatenate([acc_scr[hh] / l_scr[hh] for hh in range(heads)], axis=0)
    o_ref[...] = out_t.T


def _moba_prompt(q, k, v, bias_tiles, rel_bias, heads):
    nb, s, d = q.shape
    blk = MOBA_BLOCK
    n = HEAD_DIM
    gw = heads * n
    nblk = s // blk
    assert s % blk == 0 and gw % V7X_LANES == 0 and d % gw == 0
    qspec = pl.BlockSpec((None, blk, gw), lambda b, hp, i: (b, i, hp))
    kvspec = pl.BlockSpec((None, s, gw), lambda b, hp, i: (b, 0, hp))
    bspec = pl.BlockSpec((heads, 2, blk, blk), lambda b, hp, i: (hp, 0, 0, 0))
    return pl.pallas_call(
        _moba_prompt_body,
        grid=(nb, d // gw, nblk),
        in_specs=[pl.BlockSpec(memory_space=pltpu.SMEM), qspec, kvspec, kvspec, bspec],
        out_specs=qspec,
        out_shape=jax.ShapeDtypeStruct((nb, s, d), F32),
        scratch_shapes=[
            pltpu.VMEM((nblk, gw), F32),
            pltpu.VMEM((nblk, gw, blk), F32),
            pltpu.VMEM((heads, nblk, blk), F32),
            pltpu.VMEM((heads, 1, blk), F32),
            pltpu.VMEM((heads, 1, blk), F32),
            pltpu.VMEM((heads, n, blk), F32),
        ],
        compiler_params=_cparams(("arbitrary", "arbitrary", "arbitrary")),
        name="moba_prompt",
    )(rel_bias, q, k, v, bias_tiles)


def _moba_gate_body(pt_ref, qt_ref, k0_ref, k1_ref, top_ref, gate_scr):
    nb_ = pl.program_id(1)
    nblocks = pl.num_programs(1)
    rows = k0_ref.shape[-1] + k1_ref.shape[-1]
    kmean = jnp.sum(k0_ref[...] + k1_ref[...], axis=-1, keepdims=True) * (1.0 / rows)
    gate_scr[nb_] = jnp.sum(qt_ref[...] * kmean, axis=1, keepdims=True)

    @pl.when(nb_ == nblocks - 1)
    def _():
        g = gate_scr[...]
        idx = lax.broadcasted_iota(jnp.int32, g.shape, 0)
        for j in range(MOBA_TOP_K):
            mx = jnp.max(g, axis=0, keepdims=True)
            first = jnp.min(jnp.where(g == mx, idx, nblocks), axis=0, keepdims=True)
            top_ref[j] = first[0]
            g = jnp.where(idx == first, -jnp.inf, g)


def _moba_gate_topk(qt, cache_kt, layer, page_table, n_past_blocks):
    nb, nh, n, t = qt.shape
    page = cache_kt.shape[-1]
    ppb = MOBA_BLOCK // page
    assert ppb == 2

    def kspec(which):
        return pl.BlockSpec((None, None, nh, n, page),
                            lambda b, j, pt: (layer, pt[b, j * ppb + which], 0, 0, 0))

    grid_spec = pltpu.PrefetchScalarGridSpec(
        num_scalar_prefetch=1,
        grid=(nb, n_past_blocks),
        in_specs=[pl.BlockSpec((None, nh, n, t), lambda b, j, pt: (b, 0, 0, 0)), kspec(0), kspec(1)],
        out_specs=pl.BlockSpec((None, MOBA_TOP_K, nh, 1, t), lambda b, j, pt: (b, 0, 0, 0, 0)),
        scratch_shapes=[pltpu.VMEM((n_past_blocks, nh, 1, t), F32)],
    )
    return pl.pallas_call(
        _moba_gate_body,
        grid_spec=grid_spec,
        out_shape=jax.ShapeDtypeStruct((nb, MOBA_TOP_K, nh, 1, t), jnp.int32),
        compiler_params=_cparams(("arbitrary", "arbitrary")),
        name="moba_gate",
    )(page_table, qt, cache_kt, cache_kt)


def _moba_sample_body(top_ref, pt_ref, rb_ref, q_ref, kn_ref, vn_ref, ck_ref, cv_ref, o_ref,
                      kbuf, vbuf, sems, *, layer, n_pages, past_len, nh):
    b = pl.program_id(0)
    h = pl.program_id(1)
    t, n = q_ref.shape
    page = kbuf.shape[-1]
    ppb = MOBA_BLOCK // page
    n_sel = MOBA_TOP_K * ppb
    scale = HEAD_DIM ** -0.5

    def block_of(ti, j):
        return top_ref[((b * MOBA_TOP_K + j) * nh + h) * t + ti]

    def copies(ti, j, p):
        phys = pt_ref[b * n_pages + block_of(ti, j) * ppb + p]
        slot = (ti * MOBA_TOP_K + j) * ppb + p
        return (pltpu.make_async_copy(ck_ref.at[layer, phys, h], kbuf.at[slot], sems.at[0]),
                pltpu.make_async_copy(cv_ref.at[layer, phys, h], vbuf.at[slot], sems.at[1]))

    every = [(ti, j, p) for ti in range(t) for j in range(MOBA_TOP_K) for p in range(ppb)]
    for idx in every:
        for cp in copies(*idx):
            cp.start()
    for idx in every:
        for cp in copies(*idx):
            cp.wait()

    qs = (q_ref[...] * scale).astype(BF16)
    n_slots = t * n_sel
    n_keys = n_slots * page
    seg = n_sel * page
    kt_all = jnp.concatenate([kbuf[sl] for sl in range(n_slots)], axis=1)
    vt_all = jnp.concatenate([vbuf[sl] for sl in range(n_slots)], axis=1)
    s = _mm(qs, kt_all)
    row = lax.broadcasted_iota(jnp.int32, (t, n_keys), 0)
    col = lax.broadcasted_iota(jnp.int32, (t, n_keys), 1)
    mine = (col >= row * seg) & (col < (row + 1) * seg)
    colr = lax.broadcasted_iota(jnp.int32, (1, n_keys), 1)
    kpos = jnp.zeros((1, n_keys), jnp.int32)
    for ti in range(t):
        for j in range(MOBA_TOP_K):
            start = (ti * MOBA_TOP_K + j) * MOBA_BLOCK
            inside = (colr >= start) & (colr < start + MOBA_BLOCK)
            kpos = jnp.where(inside, block_of(ti, j) * MOBA_BLOCK + colr - start, kpos)
    rel = (past_len + row) - kpos
    s = jnp.where(mine, s + _bias_of_distance(jnp.maximum(rel, 0), rb_ref, h), NEG_BIG)
    r_o = lax.broadcasted_iota(jnp.int32, (t, t), 0)
    c_o = lax.broadcasted_iota(jnp.int32, (t, t), 1)
    rel_o = r_o - c_o
    s_own = _mm(qs, kn_ref[...], _NT) + _bias_of_distance(jnp.maximum(rel_o, 0), rb_ref, h)
    s_own = jnp.where(rel_o >= 0, s_own, NEG_BIG)
    m = jnp.maximum(jnp.max(s, axis=-1, keepdims=True), jnp.max(s_own, axis=-1, keepdims=True))
    p = jnp.exp(s - m)
    p_own = jnp.exp(s_own - m)
    denom = jnp.sum(p, axis=-1, keepdims=True) + jnp.sum(p_own, axis=-1, keepdims=True)
    o_ref[...] = (_mm(p, vt_all, _NT) + _mm(p_own, vn_ref[...])) / denom


def _moba_sample(q, k_new, v_new, top, page_table, rel_bias, cache_kt, cache_vt, layer, past_len):
    nb, nh, t, n = q.shape
    page = cache_kt.shape[-1]
    n_pages = page_table.shape[1]
    ppb = MOBA_BLOCK // page
    tok = pl.BlockSpec((None, None, t, n), lambda b, h, *_: (b, h, 0, 0))
    any_spec = pl.BlockSpec(memory_space=pl.ANY)
    grid_spec = pltpu.PrefetchScalarGridSpec(
        num_scalar_prefetch=2,
        grid=(nb, nh),
        in_specs=[pl.BlockSpec(memory_space=pltpu.SMEM), tok, tok, tok, any_spec, any_spec],
        out_specs=tok,
        scratch_shapes=[
            pltpu.VMEM((t * MOBA_TOP_K * ppb, n, page), F32),
            pltpu.VMEM((t * MOBA_TOP_K * ppb, n, page), F32),
            pltpu.SemaphoreType.DMA((2,)),
        ],
    )
    return pl.pallas_call(
        functools.partial(_moba_sample_body, layer=layer, n_pages=n_pages, past_len=past_len, nh=nh),
        grid_spec=grid_spec,
        out_shape=jax.ShapeDtypeStruct((nb, nh, t, n), F32),
        compiler_params=_cparams(("arbitrary", "arbitrary")),
        name="moba_sample",
    )(top.reshape(-1), page_table.reshape(-1), rel_bias, q, k_new, v_new, cache_kt, cache_vt)


def _tile(m, pref):
    return pref if m % pref == 0 else m


def kernel(x_prompt, x_sample, state_wkv, state_shift, cache_k, cache_v, page_table, ln1, ln2, ln_f, mlp_up, mlp_down, rwkv_mix, rwkv_w_r, rwkv_w_k, rwkv_w_v, rwkv_w_o, rwkv_w0, rwkv_w1, rwkv_w2, rwkv_a0, rwkv_a1, rwkv_a2, rwkv_g1, rwkv_g2, rwkv_k_k, rwkv_k_a, rwkv_r_k, rwkv_ln_w, rwkv_ln_b, moba_w_qkv, moba_w_o, rel_bias):
    nb, s, d = x_prompt.shape
    nbd, t, _ = x_sample.shape
    n = HEAD_DIM
    nh = d // n
    depth = ln1.shape[0]
    assert depth == 2 and rwkv_mix.shape[0] == 1 and moba_w_qkv.shape[0] == 1
    page = cache_k.shape[2]
    past_len = page_table.shape[1] * page
    assert past_len % MOBA_BLOCK == 0 and MOBA_BLOCK % page == 0
    n_past_blocks = past_len // MOBA_BLOCK
    assert n_past_blocks >= MOBA_TOP_K and s % MOBA_BLOCK == 0

    row = lambda a: a.reshape(1, -1)
    bf = lambda a: a.astype(BF16)
    lnf = row(ln_f)

    proj_w = (row(ln1[0]), rwkv_mix[0], bf(rwkv_w_r[0]), bf(rwkv_w_k[0]), bf(rwkv_w_v[0]),
              bf(rwkv_w1[0]), bf(rwkv_a1[0]), bf(rwkv_g1[0]), bf(rwkv_w2[0]), bf(rwkv_a2[0]),
              bf(rwkv_g2[0]), row(rwkv_w0[0]), row(rwkv_a0[0]))
    scan_w = (row(rwkv_k_k[0]), row(rwkv_k_a[0]), row(rwkv_r_k[0]), row(rwkv_ln_w[0]), row(rwkv_ln_b[0]))
    mlp0 = (bf(rwkv_w_o[0]), row(ln2[0]), bf(mlp_up[0]), bf(mlp_down[0]), lnf)
    mlp1 = (bf(moba_w_o[0]), row(ln2[1]), bf(mlp_up[1]), bf(mlp_down[1]), lnf)

    def rwkv_layer(x, hprev, s0, tm, chunk, tm_mlp):
        streams, shift = _rwkv_proj(x, hprev, *proj_w, tm=tm)
        z, s_out = _rwkv_scan(*streams, s0, *scan_w, chunk=chunk, heads_per_step=_tile(nh, 16))
        y = _out_mlp(x.reshape(-1, d), z.reshape(-1, d), *mlp0, tm=tm_mlp, final_norm=False)
        return y, shift, s_out

    yp, shift_p, wkv_p = rwkv_layer(x_prompt, jnp.zeros((nb, d), F32), jnp.zeros((nb, nh, n, n), F32),
                                    _tile(s, 256), _tile(s, 64), _tile(nb * s, 512))
    ys, shift_s, wkv_s = rwkv_layer(x_sample, state_shift[0], state_wkv[0], t, t, nbd * t)

    w_qkv = bf(moba_w_qkv[0])
    qp, kp, vp, kp_t, vp_t = _qkv(yp.reshape(nb, s, d), row(ln1[1]), w_qkv, _tile(s, 512), True)
    qs, ks, vs = _qkv(ys.reshape(1, nbd * t, d), row(ln1[1]), w_qkv, nbd * t, False)
    bias_tiles = _moba_bias_tiles(rel_bias, MOBA_BLOCK)
    attn_p = _moba_prompt(qp, kp, vp, bias_tiles, rel_bias, heads=_tile(nh, 4))
    per_head = lambda a: a.reshape(nbd, t, nh, n)
    cache_kt = cache_k.transpose(0, 1, 3, 4, 2)
    cache_vt = cache_v.transpose(0, 1, 3, 4, 2)
    top = _moba_gate_topk(per_head(qs).transpose(0, 2, 3, 1), cache_kt, 0, page_table, n_past_blocks)
    heads_first = lambda a: per_head(a).transpose(0, 2, 1, 3)
    attn_s = _moba_sample(heads_first(qs), heads_first(ks), heads_first(vs), top, page_table,
                          rel_bias, cache_kt, cache_vt, 0, past_len)
    attn_s = attn_s.transpose(0, 2, 1, 3).reshape(nbd * t, d)
    y_prompt = _out_mlp(yp, attn_p.reshape(-1, d), *mlp1, tm=_tile(nb * s, 512), final_norm=True)
    y_sample = _out_mlp(ys, attn_s, *mlp1, tm=nbd * t, final_norm=True)

    seq_last = lambda a: a.reshape(1, nb, nh, n, s).transpose(0, 1, 4, 2, 3)
    return (y_prompt.reshape(nb, s, d), y_sample.reshape(nbd, t, d),
            wkv_p[None], shift_p[None], seq_last(kp_t), seq_last(vp_t),
            wkv_s[None], shift_s[None],
            ks.reshape(1, nbd, t, nh, n), vs.reshape(1, nbd, t, nh, n))
```

```python
import functools
import math

import numpy as np
import jax
import jax.numpy as jnp
from jax import lax
from jax.experimental import pallas as pl
from jax.experimental.pallas import tpu as pltpu

F32 = jnp.float32
BF16 = jnp.bfloat16

HEAD_DIM = 64
GN_EPS = 64e-5
RMS_EPS = 1e-6
NEG_BIG = -1e30
MOBA_BLOCK = 256
MOBA_TOP_K = 3
N_BUCKETS = 32
MAX_DISTANCE = 128
KK_EPS = 1e-24

V7X_LANES = 128
V7X_SUBLANES = 8
V7X_VMEM_LIMIT_BYTES = 52 * 1024 * 1024

HI = lax.Precision.HIGHEST
_NN = (((1,), (0,)), ((), ()))
_NT = (((1,), (1,)), ((), ()))
_TN = (((0,), (0,)), ((), ()))


def _cparams(sem):
    return pltpu.CompilerParams(dimension_semantics=sem, vmem_limit_bytes=V7X_VMEM_LIMIT_BYTES)


def _const_spec(shape):
    nd = len(shape)
    return pl.BlockSpec(shape, lambda *_: (0,) * nd)


def _rms(x, g):
    return x * lax.rsqrt(jnp.mean(x * x, axis=-1, keepdims=True) + RMS_EPS) * g


def _mm(a, b, dims=_NN):
    return lax.dot_general(a.astype(BF16), b.astype(BF16), dims, preferred_element_type=F32)


def _rwkv_proj_body(x_ref, hprev_ref, ln_ref, mix_ref, wr_ref, wk_ref, wv_ref,
                    w1_ref, a1_ref, g1_ref, w2_ref, a2_ref, g2_ref, w0_ref, a0_ref,
                    r_ref, k_ref, v_ref, wl_ref, a_ref, g_ref, shift_ref, hbuf):
    j = pl.program_id(1)
    tm = x_ref.shape[0]
    pad = V7X_SUBLANES
    h = _rms(x_ref[...], ln_ref[...])

    @pl.when(j == 0)
    def _():
        hbuf[pad - 1:pad, :] = hprev_ref[...]

    hbuf[pad:pad + tm, :] = h
    dx = hbuf[pad - 1:pad - 1 + tm, :] - h
    hbuf[pad - 1:pad, :] = h[tm - 1:tm, :]
    shift_ref[...] = h[tm - 1:tm, :]

    def mixed(i):
        return (h + dx * mix_ref[i:i + 1, :]).astype(BF16)

    r_ref[...] = jnp.dot(mixed(0), wr_ref[...], preferred_element_type=F32)
    k_ref[...] = jnp.dot(mixed(2), wk_ref[...], preferred_element_type=F32)
    v_ref[...] = jnp.dot(mixed(3), wv_ref[...], preferred_element_type=F32)
    tw = jnp.tanh(jnp.dot(mixed(1), w1_ref[...], preferred_element_type=F32))
    wpre = w0_ref[...] + _mm(tw, w2_ref[...])
    sp = jnp.maximum(-wpre, 0.0) + jnp.log(1.0 + jnp.exp(-jnp.abs(wpre)))
    wl_ref[...] = -jnp.exp(-sp - 0.5)
    ta = jnp.dot(mixed(4), a1_ref[...], preferred_element_type=F32)
    a_ref[...] = jax.nn.sigmoid(a0_ref[...] + _mm(ta, a2_ref[...]))
    tg = jax.nn.sigmoid(jnp.dot(mixed(5), g1_ref[...], preferred_element_type=F32))
    g_ref[...] = _mm(tg, g2_ref[...])


def _rwkv_proj(x, hprev, ln, mix, wr, wk, wv, w1, a1, g1, w2, a2, g2, w0, a0, tm):
    nb, t, d = x.shape
    assert t % tm == 0
    tok = pl.BlockSpec((None, tm, d), lambda b, j: (b, j, 0))
    row = pl.BlockSpec((None, 1, d), lambda b, j: (b, 0, 0))
    ins = [x, hprev.reshape(nb, 1, d), ln, mix, wr, wk, wv, w1, a1, g1, w2, a2, g2, w0, a0]
    in_specs = [tok, row] + [_const_spec(a.shape) for a in ins[2:]]
    big = jax.ShapeDtypeStruct((nb, t, d), F32)
    outs = pl.pallas_call(
        _rwkv_proj_body,
        grid=(nb, t // tm),
        in_specs=in_specs,
        out_specs=[tok] * 6 + [row],
        out_shape=[big] * 6 + [jax.ShapeDtypeStruct((nb, 1, d), F32)],
        scratch_shapes=[pltpu.VMEM((tm + 2 * V7X_SUBLANES, d), F32)],
        compiler_params=_cparams(("arbitrary", "arbitrary")),
        name="rwkv_proj",
    )(*ins)
    return outs[:6], outs[6].reshape(nb, d)


_INV_BASE = 8


def _rwkv_pairs_chunk(r, k, v, wl, a, g, sbd, kkp, kap, rkp, lnw, lnb, m):
    c = r[0].shape[0]
    ch0, th0, strict, incl, eye_cat, tril_bf, bd128, n_square, rowb, colb, levels = m
    zero = jnp.zeros((), F32)
    each = lambda fn, *ls: [fn(*xs) for xs in zip(*ls)]

    def seg_sum(x):
        s0 = jnp.sum(jnp.where(ch0, x, zero), axis=-1, keepdims=True)
        s1 = jnp.sum(jnp.where(ch0, zero, x), axis=-1, keepdims=True)
        return jnp.where(ch0, s0, s1)

    def bd_chan(x):
        return jnp.concatenate([jnp.where(ch0, x, zero), jnp.where(ch0, zero, x)], axis=0)

    def bd_chan_swapped(x):
        return jnp.concatenate([jnp.where(ch0, zero, x), jnp.where(ch0, x, zero)], axis=0)

    def bd_time(x):
        return jnp.concatenate([jnp.where(th0, x, zero), jnp.where(th0, zero, x)], axis=0)

    def normalised(k_, kkp_):
        kk_ = k_ * kkp_
        return kk_ * lax.rsqrt(jnp.maximum(seg_sum(kk_ * kk_), KK_EPS))

    def cumulative(wl_):
        w_hi = wl_.astype(BF16)
        w_r1 = wl_ - w_hi.astype(F32)
        w_mid = w_r1.astype(BF16)
        w_lo = (w_r1 - w_mid.astype(F32)).astype(BF16)
        return (jnp.dot(tril_bf, w_hi, preferred_element_type=F32)
                + (jnp.dot(tril_bf, w_mid, preferred_element_type=F32)
                   + jnp.dot(tril_bf, w_lo, preferred_element_type=F32)))

    kk = each(normalised, k, kkp)
    k2 = each(lambda k_, a_, kap_: k_ * (1.0 + (a_ - 1.0) * kap_), k, a, kap)
    b = each(lambda kk_, a_: kk_ * a_, kk, a)
    cum = each(cumulative, wl)
    cum_end = [x[c - 1:c, :] for x in cum]
    lhs = each(lambda kk_, r_, cum_, wl_: jnp.concatenate(
        [kk_ * jnp.exp(cum_ - wl_), r_ * jnp.exp(cum_)], axis=0), kk, r, cum, wl)
    inv_w = [jnp.exp(-x) for x in cum]
    bt = each(lambda b_, w_: b_ * w_, b, inv_w)
    kt = each(lambda k_, w_: k_ * w_, k2, inv_w)
    to_end = each(lambda e_, cum_: jnp.exp(e_ - cum_), cum_end, cum)
    rhs_end = each(lambda b_, k_, w_: jnp.concatenate([b_ * w_, k_ * w_], axis=0), b, k2, to_end)

    g0 = each(lambda l_, bt_, kt_: _mm(jnp.where(ch0, l_, zero), jnp.concatenate([bt_, kt_], axis=0), _NT),
              lhs, bt, kt)
    g1 = each(lambda l_, bt_, kt_: _mm(jnp.where(ch0, zero, l_), jnp.concatenate([kt_, bt_], axis=0), _NT),
              lhs, bt, kt)
    mab_c = each(lambda x, y: jnp.where(strict, jnp.where(th0, x[:c], y[:c]), zero), g0, g1)
    mak_s = each(lambda x, y: jnp.where(strict, jnp.where(th0, y[:c], x[:c]), zero), g0, g1)
    nab_c = each(lambda x, y: jnp.where(incl, jnp.where(th0, x[c:], y[c:]), zero), g0, g1)
    nak_s = each(lambda x, y: jnp.where(incl, jnp.where(th0, y[c:], x[c:]), zero), g0, g1)

    ls = each(lambda l_, s_: _mm(l_, s_, _NT), lhs, sbd)
    mn = each(lambda m_, n_, v_: _mm(jnp.concatenate([m_, n_], axis=0), bd_chan_swapped(v_)),
              mak_s, nak_s, v)

    xp = [jnp.where(rowb == colb, -x, zero) for x in mab_c]
    p = [eye_cat + x for x in xp]
    for _ in range(n_square):
        xp = each(lambda x: _mm(x, bd_time(x)), xp)
        p = each(lambda p_, x: p_ + _mm(p_, bd_time(x)), p, xp)
    for lev in levels:
        half = each(lambda p_, m_: _mm(p_, bd_time(jnp.where(lev, m_, zero))), p, mab_c)
        p = each(lambda p_, h_: p_ - _mm(h_, bd_time(p_)), p, half)

    sa = each(lambda p_, ls_, mn_: -_mm(p_, bd_chan(ls_[:c] + mn_[:c])), p, ls, mn)
    y = each(lambda ls_, n_, sa_, mn_: ls_[c:] + _mm(n_, bd_chan(sa_)) + mn_[c:], ls, nab_c, sa, mn)
    upd = each(lambda sa_, v_, e_: _mm(jnp.concatenate([sa_, v_], axis=0), e_, _TN), sa, v, rhs_end)
    sbd_new = each(lambda s_, e_, u_: s_ * jnp.exp(e_) + jnp.where(bd128, u_, zero), sbd, cum_end, upd)

    inv_n = 1.0 / HEAD_DIM

    def finish(y_, r_, k2_, v_, g_, rkp_, lnw_, lnb_):
        yc = y_ - seg_sum(y_) * inv_n
        var = seg_sum(yc * yc) * inv_n
        yn = yc * lax.rsqrt(var + GN_EPS) * lnw_ + lnb_
        return (yn + seg_sum(r_ * k2_ * rkp_) * v_) * g_

    return each(finish, y, r, k2, v, g, rkp, lnw, lnb), sbd_new


def _rwkv_scan_body(r_ref, k_ref, v_ref, wl_ref, a_ref, g_ref, s0_ref,
                    kkp_ref, kap_ref, rkp_ref, lnw_ref, lnb_ref,
                    z_ref, sout_ref, s_scr):
    ci = pl.program_id(2)
    c = r_ref.shape[0]
    n = HEAD_DIM
    w = 2 * n
    pairs = r_ref.shape[1] // w
    zpad = jnp.zeros((n, n), F32)

    @pl.when(ci == 0)
    def _():
        for pi in range(pairs):
            top = jnp.concatenate([s0_ref[2 * pi], zpad], axis=1)
            bot = jnp.concatenate([zpad, s0_ref[2 * pi + 1]], axis=1)
            s_scr[pi] = jnp.concatenate([top, bot], axis=0)

    lane = lax.broadcasted_iota(jnp.int32, (1, w), 1)
    rowt = lax.broadcasted_iota(jnp.int32, (c, 2 * c), 0)
    colt = lax.broadcasted_iota(jnp.int32, (c, 2 * c), 1)
    th0 = colt < c
    tcol = jnp.where(th0, colt, colt - c)
    row_c = lax.broadcasted_iota(jnp.int32, (c, c), 0)
    col_c = lax.broadcasted_iota(jnp.int32, (c, c), 1)
    r128 = lax.broadcasted_iota(jnp.int32, (w, w), 0)
    c128 = lax.broadcasted_iota(jnp.int32, (w, w), 1)
    base = min(c, _INV_BASE)
    base_log = int(math.log2(base))
    assert 1 << base_log == base and c % base == 0 and (c // base) & (c // base - 1) == 0
    levels = []
    size_log = base_log
    while (1 << size_log) < c:
        rb = lax.shift_right_logical(rowt, size_log)
        cb = lax.shift_right_logical(tcol, size_log)
        levels.append(((rb & 1) == 1) & (cb == rb - 1))
        size_log += 1
    masks = (lane < n, th0, tcol < rowt, tcol <= rowt, jnp.where(tcol == rowt, 1.0, 0.0).astype(F32),
             jnp.where(col_c <= row_c, 1.0, 0.0).astype(BF16), (r128 < n) == (c128 < n),
             base_log - 1, lax.shift_right_logical(rowt, base_log),
             lax.shift_right_logical(tcol, base_log), levels)
    lanes = [slice(pi * w, (pi + 1) * w) for pi in range(pairs)]
    per_pair = lambda ref: [ref[:, sl] for sl in lanes]
    z, s_new = _rwkv_pairs_chunk(
        per_pair(r_ref), per_pair(k_ref), per_pair(v_ref), per_pair(wl_ref), per_pair(a_ref),
        per_pair(g_ref), [s_scr[pi] for pi in range(pairs)], per_pair(kkp_ref), per_pair(kap_ref),
        per_pair(rkp_ref), per_pair(lnw_ref), per_pair(lnb_ref), masks)
    for pi in range(pairs):
        z_ref[:, lanes[pi]] = z[pi]
        s_scr[pi] = s_new[pi]

    @pl.when(ci == pl.num_programs(2) - 1)
    def _():
        for pi in range(pairs):
            s = s_scr[pi]
            sout_ref[2 * pi] = s[:n, :n]
            sout_ref[2 * pi + 1] = s[n:, n:]


def _rwkv_scan(r, k, v, wl, a, g, s0, kkp, kap, rkp, lnw, lnb, chunk, heads_per_step):
    nb, t, d = r.shape
    n = HEAD_DIM
    nh = d // n
    gw = heads_per_step * n
    assert t % chunk == 0 and nh % heads_per_step == 0 and gw % V7X_LANES == 0
    tok = pl.BlockSpec((None, chunk, gw), lambda b, hg, ci: (b, ci, hg))
    st = pl.BlockSpec((None, heads_per_step, n, n), lambda b, hg, ci: (b, hg, 0, 0))
    par = pl.BlockSpec((1, gw), lambda b, hg, ci: (0, hg))
    z, s_out = pl.pallas_call(
        _rwkv_scan_body,
        grid=(nb, nh // heads_per_step, t // chunk),
        in_specs=[tok] * 6 + [st] + [par] * 5,
        out_specs=[tok, st],
        out_shape=[jax.ShapeDtypeStruct((nb, t, d), F32),
                   jax.ShapeDtypeStruct((nb, nh, n, n), F32)],
        scratch_shapes=[pltpu.VMEM((heads_per_step // 2, 2 * n, 2 * n), F32)],
        compiler_params=_cparams(("arbitrary", "arbitrary", "arbitrary")),
        name="rwkv_scan",
    )(r, k, v, wl, a, g, s0, kkp, kap, rkp, lnw, lnb)
    return z, s_out


def _out_mlp_body(x_ref, z_ref, wo_ref, ln2_ref, up_ref, dn_ref, lnf_ref, y_ref, *, ff_chunk, final_norm):
    yp = x_ref[...] + jnp.dot(z_ref[...].astype(BF16), wo_ref[...], preferred_element_type=F32)
    h = _rms(yp, ln2_ref[...]).astype(BF16)
    acc = yp
    for c0 in range(0, up_ref.shape[1], ff_chunk):
        u = jnp.dot(h, up_ref[:, c0:c0 + ff_chunk], preferred_element_type=F32)
        u = jnp.square(jnp.maximum(u, 0.0)).astype(BF16)
        acc = acc + jnp.dot(u, dn_ref[c0:c0 + ff_chunk, :], preferred_element_type=F32)
    y_ref[...] = _rms(acc, lnf_ref[...]) if final_norm else acc


def _out_mlp(x, z, wo, ln2, up, dn, lnf, tm, final_norm):
    m, d = x.shape
    assert m % tm == 0
    tok = pl.BlockSpec((tm, d), lambda i: (i, 0))
    ins = [x, z, wo, ln2, up, dn, lnf]
    return pl.pallas_call(
        functools.partial(_out_mlp_body, ff_chunk=1024, final_norm=final_norm),
        grid=(m // tm,),
        in_specs=[tok, tok] + [_const_spec(a.shape) for a in ins[2:]],
        out_specs=tok,
        out_shape=jax.ShapeDtypeStruct((m, d), F32),
        compiler_params=_cparams(("arbitrary",)),
        name="out_mlp",
    )(*ins)


def _qkv_body(x_ref, ln_ref, w_ref, q_ref, k_ref, v_ref, *t_refs):
    d = x_ref.shape[1]
    h = _rms(x_ref[...], ln_ref[...]).astype(BF16)
    q_ref[...] = jnp.dot(h, w_ref[:, 0:d], preferred_element_type=F32)
    k = jnp.dot(h, w_ref[:, d:2 * d], preferred_element_type=F32)
    v = jnp.dot(h, w_ref[:, 2 * d:3 * d], preferred_element_type=F32)
    k_ref[...] = k
    v_ref[...] = v
    if t_refs:
        kt_ref, vt_ref = t_refs
        kt_ref[...] = k.T
        vt_ref[...] = v.T


def _qkv(x, ln, w, tm, transposed):
    nb, s, d = x.shape
    assert s % tm == 0
    tok = pl.BlockSpec((None, tm, d), lambda b, j: (b, j, 0))
    tok_t = pl.BlockSpec((None, d, tm), lambda b, j: (b, 0, j))
    sds = jax.ShapeDtypeStruct((nb, s, d), F32)
    sds_t = jax.ShapeDtypeStruct((nb, d, s), F32)
    return pl.pallas_call(
        _qkv_body,
        grid=(nb, s // tm),
        in_specs=[tok, _const_spec(ln.shape), _const_spec(w.shape)],
        out_specs=[tok] * 3 + ([tok_t] * 2 if transposed else []),
        out_shape=[sds] * 3 + ([sds_t] * 2 if transposed else []),
        compiler_params=_cparams(("arbitrary", "arbitrary")),
        name="moba_qkv",
    )(x, ln, w)


def _bucket_starts():
    n = np.arange(MAX_DISTANCE, dtype=np.float32)
    max_exact = N_BUCKETS // 2
    nf = np.maximum(n, 1.0).astype(np.float32)
    large = max_exact + (np.log(nf / np.float32(max_exact)) / np.float32(math.log(MAX_DISTANCE / max_exact))
                         * np.float32(N_BUCKETS - max_exact)).astype(np.int32)
    bucket = np.where(n < max_exact, n.astype(np.int32), np.minimum(large, N_BUCKETS - 1))
    starts = [int(np.argmax(bucket >= b)) if (bucket >= b).any() else MAX_DISTANCE
              for b in range(N_BUCKETS)]
    starts[N_BUCKETS - 1] = min(starts[N_BUCKETS - 1], MAX_DISTANCE)
    return starts


_BUCKET_STARTS = _bucket_starts()


def _bias_of_distance(rel, rb_ref, head):
    bias = jnp.full(rel.shape, rb_ref[0, head], F32)
    for b in range(1, N_BUCKETS):
        bias = jnp.where(rel >= _BUCKET_STARTS[b], rb_ref[b, head], bias)
    return bias


def _moba_bias_body(rb_ref, o_ref):
    h = pl.program_id(0)
    blk = o_ref.shape[-1]
    kj = lax.broadcasted_iota(jnp.int32, (blk, blk), 0)
    qi = lax.broadcasted_iota(jnp.int32, (blk, blk), 1)
    rel = qi - kj
    own = _bias_of_distance(jnp.maximum(rel, 0), rb_ref, h)
    o_ref[0] = jnp.where(rel >= 0, own, NEG_BIG)
    o_ref[1] = _bias_of_distance(rel + blk, rb_ref, h)


def _moba_bias_tiles(rel_bias, blk):
    nh = rel_bias.shape[1]
    return pl.pallas_call(
        _moba_bias_body,
        grid=(nh,),
        in_specs=[pl.BlockSpec(memory_space=pltpu.SMEM)],
        out_specs=pl.BlockSpec((None, 2, blk, blk), lambda h: (h, 0, 0, 0)),
        out_shape=jax.ShapeDtypeStruct((nh, 2, blk, blk), F32),
        compiler_params=_cparams(("arbitrary",)),
        name="moba_bias",
    )(rel_bias)


_ONES_ROWS = 16


def _moba_prompt_body(rb_ref, q_ref, k_ref, v_ref, bias_ref, o_ref,
                      kmean_scr, kb_scr, vt_scr, sel_scr, m_scr, l_scr, acc_scr):
    i = pl.program_id(2)
    hp = pl.program_id(1)
    blk = q_ref.shape[0]
    n = HEAD_DIM
    nblk = k_ref.shape[0] // blk
    heads = q_ref.shape[1] // n
    scale = HEAD_DIM ** -0.5

    @pl.when(i == 0)
    def _():
        ones = jnp.ones((_ONES_ROWS, blk), BF16)
        for b in range(nblk):
            kb = k_ref[b * blk:(b + 1) * blk, :]
            kmean_scr[b:b + 1, :] = jnp.mean(kb, axis=0, keepdims=True)
            kb_scr[b * blk:(b + 1) * blk, :] = kb.astype(BF16)
            vt = v_ref[b * blk:(b + 1) * blk, :].T.astype(BF16)
            for hh in range(heads):
                vt_scr[b, hh] = jnp.concatenate([vt[hh * n:(hh + 1) * n, :], ones], axis=0)

    lane = lax.broadcasted_iota(jnp.int32, (1, heads * n), 1)
    q = q_ref[...]
    blk_id = lax.broadcasted_iota(jnp.int32, (nblk, blk), 0)

    hs = range(heads)

    def attend(k_blk, vt_blk, bias, sel_row, first):
        if sel_row is not None and bias[0].ndim == 0:
            bias = [jnp.where(sel_row[hh] > 0, bias[hh], NEG_BIG) for hh in hs]
            sel_row = None
        s = [lax.dot_general(k_blk, qm[hh], _NT, preferred_element_type=F32) + bias[hh] for hh in hs]
        if sel_row is not None:
            s = [jnp.where(sel_row[hh] > 0, s[hh], NEG_BIG) for hh in hs]
        m_new = [jnp.max(s[hh], axis=0, keepdims=True) for hh in hs]
        if not first:
            m_old = [m_scr[hh] for hh in hs]
            m_new = [jnp.maximum(m_old[hh], m_new[hh]) for hh in hs]
        p = [jnp.exp((s[hh] - m_new[hh]).astype(BF16)) for hh in hs]
        pv = [jnp.dot(vt_blk[hh], p[hh], preferred_element_type=F32) for hh in hs]
        for hh in hs:
            l_blk = pv[hh][n:n + 1, :]
            if first:
                l_scr[hh] = l_blk
                acc_scr[hh] = pv[hh][:n]
            else:
                alpha = jnp.exp(m_old[hh] - m_new[hh])
                l_scr[hh] = l_scr[hh] * alpha + l_blk
                acc_scr[hh] = acc_scr[hh] * alpha + pv[hh][:n]
            m_scr[hh] = m_new[hh]

    in_head = [(lane >= hh * n) & (lane < (hh + 1) * n) for hh in hs]
    qm = [(jnp.where(in_head[hh], q, 0.0) * scale).astype(BF16) for hh in hs]
    kmean = kmean_scr[...]
    kmean_heads = jnp.concatenate([jnp.where(in_head[hh], kmean, 0.0) for hh in hs], axis=0)
    gate_all = lax.dot_general(kmean_heads, q, _NT, precision=HI, preferred_element_type=F32)
    for hh in hs:
        gate = gate_all[hh * nblk:(hh + 1) * nblk, :]
        cnt = jnp.zeros((nblk, blk), jnp.int32)
        for mb in range(nblk - 1):
            gm = gate[mb:mb + 1, :]
            beats = (gm > gate) | ((gm == gate) & (mb < blk_id))
            cnt = cnt + jnp.where(beats, (mb < i).astype(jnp.int32), 0)
        sel = (blk_id < i) & (cnt < MOBA_TOP_K)
        sel_scr[hh] = jnp.where(sel, 1.0, 0.0)

    def k_block(b):
        return kb_scr[pl.ds(pl.multiple_of(b * blk, blk), blk), :]

    attend(k_block(i), vt_scr[i], [bias_ref[hh, 0] for hh in hs], None, True)

    @pl.when(i >= 1)
    def _():
        attend(k_block(i - 1), vt_scr[i - 1], [bias_ref[hh, 1] for hh in hs],
               [sel_scr[hh, pl.ds(i - 1, 1), :] for hh in hs], False)

    def far_block(b, carry):
        attend(k_block(b), vt_scr[b], [rb_ref[N_BUCKETS - 1, hp * heads + hh] for hh in hs],
               [sel_scr[hh, pl.ds(b, 1), :] for hh in hs], False)
        return carry

    lax.fori_loop(0, jnp.maximum(i - 1, 0), far_block, 0)

    out_t = jnp.concatenate([acc_scr[hh] / l_scr[hh] for hh in range(heads)], axis=0)
    o_ref[...] = out_t.T


def _moba_prompt(q, k, v, bias_tiles, rel_bias, heads):
    nb, s, d = q.shape
    blk = MOBA_BLOCK
    n = HEAD_DIM
    gw = heads * n
    nblk = s // blk
    assert s % blk == 0 and gw % V7X_LANES == 0 and d % gw == 0
    qspec = pl.BlockSpec((None, blk, gw), lambda b, hp, i: (b, i, hp))
    kvspec = pl.BlockSpec((None, s, gw), lambda b, hp, i: (b, 0, hp))
    bspec = pl.BlockSpec((heads, 2, blk, blk), lambda b, hp, i: (hp, 0, 0, 0))
    return pl.pallas_call(
        _moba_prompt_body,
        grid=(nb, d // gw, nblk),
        in_specs=[pl.BlockSpec(memory_space=pltpu.SMEM), qspec, kvspec, kvspec, bspec],
        out_specs=qspec,
        out_shape=jax.ShapeDtypeStruct((nb, s, d), F32),
        scratch_shapes=[
            pltpu.VMEM((nblk, gw), F32),
            pltpu.VMEM((s, gw), BF16),
            pltpu.VMEM((nblk, heads, n + _ONES_ROWS, blk), BF16),
            pltpu.VMEM((heads, nblk, blk), F32),
            pltpu.VMEM((heads, 1, blk), F32),
            pltpu.VMEM((heads, 1, blk), F32),
            pltpu.VMEM((heads, n, blk), F32),
        ],
        compiler_params=_cparams(("arbitrary", "arbitrary", "arbitrary")),
        name="moba_prompt",
    )(rel_bias, q, k, v, bias_tiles)


_GATE_BLOCKS_PER_STEP = 4


def _moba_gate_body(pt_ref, qt_ref, *rest, ppb):
    top_ref, gate_scr = rest[-2:]
    page_refs = rest[:-2]
    step = pl.program_id(1)
    blocks_per_step = len(page_refs) // ppb
    nblocks = gate_scr.shape[0]
    rows = ppb * page_refs[0].shape[-1]
    for bi in range(blocks_per_step):
        ksum = page_refs[bi * ppb][...]
        for w in range(1, ppb):
            ksum = ksum + page_refs[bi * ppb + w][...]
        kmean = jnp.sum(ksum, axis=-1, keepdims=True) * (1.0 / rows)
        gate_scr[step * blocks_per_step + bi] = jnp.sum(qt_ref[...] * kmean, axis=1, keepdims=True)

    @pl.when(step == pl.num_programs(1) - 1)
    def _():
        g = gate_scr[...]
        idx = lax.broadcasted_iota(jnp.int32, g.shape, 0)
        for j in range(MOBA_TOP_K):
            mx = jnp.max(g, axis=0, keepdims=True)
            first = jnp.min(jnp.where(g == mx, idx, nblocks), axis=0, keepdims=True)
            top_ref[j] = first[0]
            g = jnp.where(idx == first, -jnp.inf, g)


def _moba_gate_topk(qt, cache_kt, layer, page_table, n_past_blocks):
    nb, nh, n, t = qt.shape
    page = cache_kt.shape[-1]
    ppb = MOBA_BLOCK // page
    pages_per_step = _tile(n_past_blocks, _GATE_BLOCKS_PER_STEP) * ppb

    def kspec(which):
        return pl.BlockSpec((None, None, nh, n, page),
                            lambda b, j, pt: (layer, pt[b, j * pages_per_step + which], 0, 0, 0))

    grid_spec = pltpu.PrefetchScalarGridSpec(
        num_scalar_prefetch=1,
        grid=(nb, n_past_blocks * ppb // pages_per_step),
        in_specs=[pl.BlockSpec((None, nh, n, t), lambda b, j, pt: (b, 0, 0, 0))]
        + [kspec(w) for w in range(pages_per_step)],
        out_specs=pl.BlockSpec((None, MOBA_TOP_K, nh, 1, t), lambda b, j, pt: (b, 0, 0, 0, 0)),
        scratch_shapes=[pltpu.VMEM((n_past_blocks, nh, 1, t), F32)],
    )
    return pl.pallas_call(
        functools.partial(_moba_gate_body, ppb=ppb),
        grid_spec=grid_spec,
        out_shape=jax.ShapeDtypeStruct((nb, MOBA_TOP_K, nh, 1, t), jnp.int32),
        compiler_params=_cparams(("arbitrary", "arbitrary")),
        name="moba_gate",
    )(page_table, qt, *([cache_kt] * pages_per_step))


def _moba_sample_body(top_ref, pt_ref, rb_ref, q_ref, kn_ref, vn_ref, ck_ref, cv_ref, o_ref,
                      kbuf, vbuf, sems, *, layer, n_pages, past_len, nh):
    b = pl.program_id(0)
    h = pl.program_id(1)
    t, n = q_ref.shape
    page = kbuf.shape[-1]
    ppb = MOBA_BLOCK // page
    n_sel = MOBA_TOP_K * ppb
    scale = HEAD_DIM ** -0.5

    step = b * nh + h
    cur = lax.rem(step, 2)

    def block_of(ti, j, bb=b, hh=h):
        return top_ref[((bb * MOBA_TOP_K + j) * nh + hh) * t + ti]

    def copies(bb, hh, half, ti, j, p):
        phys = pt_ref[bb * n_pages + block_of(ti, j, bb, hh) * ppb + p]
        slot = (ti * MOBA_TOP_K + j) * ppb + p
        return (pltpu.make_async_copy(ck_ref.at[layer, phys, hh], kbuf.at[half, slot], sems.at[half, 0]),
                pltpu.make_async_copy(cv_ref.at[layer, phys, hh], vbuf.at[half, slot], sems.at[half, 1]))

    every = [(ti, j, p) for ti in range(t) for j in range(MOBA_TOP_K) for p in range(ppb)]

    def start_all(bb, hh, half):
        for idx in every:
            for cp in copies(bb, hh, half, *idx):
                cp.start()

    @pl.when(step == 0)
    def _():
        start_all(b, h, cur)

    @pl.when(step + 1 < pl.num_programs(0) * nh)
    def _():
        wraps = h == nh - 1
        start_all(jnp.where(wraps, b + 1, b), jnp.where(wraps, 0, h + 1), 1 - cur)

    for idx in every:
        for cp in copies(b, h, cur, *idx):
            cp.wait()

    qs = (q_ref[...] * scale).astype(BF16)
    n_slots = t * n_sel
    n_keys = n_slots * page
    seg = n_sel * page
    kt_all = jnp.concatenate([kbuf[cur, sl] for sl in range(n_slots)], axis=1)
    vt_all = jnp.concatenate([vbuf[cur, sl] for sl in range(n_slots)], axis=1)
    s_all = _mm(qs, kt_all)
    row = lax.broadcasted_iota(jnp.int32, (t, seg), 0)
    offs = lax.broadcasted_iota(jnp.int32, (1, MOBA_BLOCK), 1)
    s = jnp.zeros((t, seg), F32)
    kpos = jnp.zeros((t, seg), jnp.int32)
    for ti in range(t):
        s = jnp.where(row == ti, s_all[:, ti * seg:(ti + 1) * seg], s)
        pos = jnp.concatenate([block_of(ti, j) * MOBA_BLOCK + offs for j in range(MOBA_TOP_K)], axis=1)
        kpos = jnp.where(row == ti, pos, kpos)
    rel = (past_len + row) - kpos
    s = s + _bias_of_distance(jnp.maximum(rel, 0), rb_ref, h)
    r_o = lax.broadcasted_iota(jnp.int32, (t, t), 0)
    c_o = lax.broadcasted_iota(jnp.int32, (t, t), 1)
    rel_o = r_o - c_o
    s_own = _mm(qs, kn_ref[...], _NT) + _bias_of_distance(jnp.maximum(rel_o, 0), rb_ref, h)
    s_own = jnp.where(rel_o >= 0, s_own, NEG_BIG)
    m = jnp.maximum(jnp.max(s, axis=-1, keepdims=True), jnp.max(s_own, axis=-1, keepdims=True))
    p = jnp.exp(s - m)
    p_own = jnp.exp(s_own - m)
    denom = jnp.sum(p, axis=-1, keepdims=True) + jnp.sum(p_own, axis=-1, keepdims=True)
    p_all = jnp.concatenate([jnp.where(row == ti, p, 0.0) for ti in range(t)], axis=1)
    o_ref[...] = (_mm(p_all, vt_all, _NT) + _mm(p_own, vn_ref[...])) / denom


def _moba_sample(q, k_new, v_new, top, page_table, rel_bias, cache_kt, cache_vt, layer, past_len):
    nb, nh, t, n = q.shape
    page = cache_kt.shape[-1]
    n_pages = page_table.shape[1]
    ppb = MOBA_BLOCK // page
    tok = pl.BlockSpec((None, None, t, n), lambda b, h, *_: (b, h, 0, 0))
    any_spec = pl.BlockSpec(memory_space=pl.ANY)
    grid_spec = pltpu.PrefetchScalarGridSpec(
        num_scalar_prefetch=2,
        grid=(nb, nh),
        in_specs=[pl.BlockSpec(memory_space=pltpu.SMEM), tok, tok, tok, any_spec, any_spec],
        out_specs=tok,
        scratch_shapes=[
            pltpu.VMEM((2, t * MOBA_TOP_K * ppb, n, page), F32),
            pltpu.VMEM((2, t * MOBA_TOP_K * ppb, n, page), F32),
            pltpu.SemaphoreType.DMA((2, 2)),
        ],
    )
    return pl.pallas_call(
        functools.partial(_moba_sample_body, layer=layer, n_pages=n_pages, past_len=past_len, nh=nh),
        grid_spec=grid_spec,
        out_shape=jax.ShapeDtypeStruct((nb, nh, t, n), F32),
        compiler_params=_cparams(("arbitrary", "arbitrary")),
        name="moba_sample",
    )(top.reshape(-1), page_table.reshape(-1), rel_bias, q, k_new, v_new, cache_kt, cache_vt)


def _tile(m, pref):
    return pref if m % pref == 0 else m


def kernel(x_prompt, x_sample, state_wkv, state_shift, cache_k, cache_v, page_table, ln1, ln2, ln_f, mlp_up, mlp_down, rwkv_mix, rwkv_w_r, rwkv_w_k, rwkv_w_v, rwkv_w_o, rwkv_w0, rwkv_w1, rwkv_w2, rwkv_a0, rwkv_a1, rwkv_a2, rwkv_g1, rwkv_g2, rwkv_k_k, rwkv_k_a, rwkv_r_k, rwkv_ln_w, rwkv_ln_b, moba_w_qkv, moba_w_o, rel_bias):
    nb, s, d = x_prompt.shape
    nbd, t, _ = x_sample.shape
    n = HEAD_DIM
    nh = d // n
    depth = ln1.shape[0]
    assert depth == 2 and rwkv_mix.shape[0] == 1 and moba_w_qkv.shape[0] == 1
    page = cache_k.shape[2]
    past_len = page_table.shape[1] * page
    assert past_len % MOBA_BLOCK == 0 and MOBA_BLOCK % page == 0
    n_past_blocks = past_len // MOBA_BLOCK
    assert n_past_blocks >= MOBA_TOP_K and s % MOBA_BLOCK == 0

    row = lambda a: a.reshape(1, -1)
    bf = lambda a: a.astype(BF16)
    lnf = row(ln_f)

    proj_w = (row(ln1[0]), rwkv_mix[0], bf(rwkv_w_r[0]), bf(rwkv_w_k[0]), bf(rwkv_w_v[0]),
              bf(rwkv_w1[0]), bf(rwkv_a1[0]), bf(rwkv_g1[0]), bf(rwkv_w2[0]), bf(rwkv_a2[0]),
              bf(rwkv_g2[0]), row(rwkv_w0[0]), row(rwkv_a0[0]))
    scan_w = (row(rwkv_k_k[0]), row(rwkv_k_a[0]), row(rwkv_r_k[0]), row(rwkv_ln_w[0]), row(rwkv_ln_b[0]))
    mlp0 = (bf(rwkv_w_o[0]), row(ln2[0]), bf(mlp_up[0]), bf(mlp_down[0]), lnf)
    mlp1 = (bf(moba_w_o[0]), row(ln2[1]), bf(mlp_up[1]), bf(mlp_down[1]), lnf)

    def rwkv_layer(x, hprev, s0, tm, chunk, tm_mlp):
        streams, shift = _rwkv_proj(x, hprev, *proj_w, tm=tm)
        z, s_out = _rwkv_scan(*streams, s0, *scan_w, chunk=chunk, heads_per_step=_tile(nh, 16))
        y = _out_mlp(x.reshape(-1, d), z.reshape(-1, d), *mlp0, tm=tm_mlp, final_norm=False)
        return y, shift, s_out

    yp, shift_p, wkv_p = rwkv_layer(x_prompt, jnp.zeros((nb, d), F32), jnp.zeros((nb, nh, n, n), F32),
                                    _tile(s, 256), _tile(s, 64), _tile(nb * s, 512))
    ys, shift_s, wkv_s = rwkv_layer(x_sample, state_shift[0], state_wkv[0], t, t, nbd * t)

    w_qkv = bf(moba_w_qkv[0])
    qp, kp, vp, kp_t, vp_t = _qkv(yp.reshape(nb, s, d), row(ln1[1]), w_qkv, _tile(s, 512), True)
    qs, ks, vs = _qkv(ys.reshape(1, nbd * t, d), row(ln1[1]), w_qkv, nbd * t, False)
    bias_tiles = _moba_bias_tiles(rel_bias, MOBA_BLOCK)
    attn_p = _moba_prompt(qp, kp, vp, bias_tiles, rel_bias, heads=_tile(nh, 4))
    per_head = lambda a: a.reshape(nbd, t, nh, n)
    cache_kt = cache_k.transpose(0, 1, 3, 4, 2)
    cache_vt = cache_v.transpose(0, 1, 3, 4, 2)
    top = _moba_gate_topk(per_head(qs).transpose(0, 2, 3, 1), cache_kt, 0, page_table, n_past_blocks)
    heads_first = lambda a: per_head(a).transpose(0, 2, 1, 3)
    attn_s = _moba_sample(heads_first(qs), heads_first(ks), heads_first(vs), top, page_table,
                          rel_bias, cache_kt, cache_vt, 0, past_len)
    attn_s = attn_s.transpose(0, 2, 1, 3).reshape(nbd * t, d)
    y_prompt = _out_mlp(yp, attn_p.reshape(-1, d), *mlp1, tm=_tile(nb * s, 512), final_norm=True)
    y_sample = _out_mlp(ys, attn_s, *mlp1, tm=nbd * t, final_norm=True)

    seq_last = lambda a: a.reshape(1, nb, nh, n, s).transpose(0, 1, 4, 2, 3)
    return (y_prompt.reshape(nb, s, d), y_sample.reshape(nbd, t, d),
            wkv_p[None], shift_p[None], seq_last(kp_t), seq_last(vp_t),
            wkv_s[None], shift_s[None],
            ks.reshape(1, nbd, t, nh, n), vs.reshape(1, nbd, t, nh, n))
```

```python
import functools
import math

import numpy as np
import jax
import jax.numpy as jnp
from jax import lax
from jax.experimental import pallas as pl
from jax.experimental.pallas import tpu as pltpu

F32 = jnp.float32
BF16 = jnp.bfloat16

HEAD_DIM = 64
GN_EPS = 64e-5
RMS_EPS = 1e-6
NEG_BIG = -1e30
MOBA_BLOCK = 256
MOBA_TOP_K = 3
N_BUCKETS = 32
MAX_DISTANCE = 128
KK_EPS = 1e-24

V7X_LANES = 128
V7X_SUBLANES = 8
V7X_VMEM_LIMIT_BYTES = 52 * 1024 * 1024

HI = lax.Precision.HIGHEST
_NN = (((1,), (0,)), ((), ()))
_NT = (((1,), (1,)), ((), ()))
_TN = (((0,), (0,)), ((), ()))


def _cparams(sem):
    return pltpu.CompilerParams(dimension_semantics=sem, vmem_limit_bytes=V7X_VMEM_LIMIT_BYTES)


def _const_spec(shape):
    nd = len(shape)
    return pl.BlockSpec(shape, lambda *_: (0,) * nd)


def _rms(x, g):
    return x * lax.rsqrt(jnp.mean(x * x, axis=-1, keepdims=True) + RMS_EPS) * g


def _mm(a, b, dims=_NN):
    return lax.dot_general(a.astype(BF16), b.astype(BF16), dims, preferred_element_type=F32)


def _rwkv_proj_body(x_ref, hprev_ref, ln_ref, mix_ref, wr_ref, wk_ref, wv_ref,
                    w1_ref, a1_ref, g1_ref, w2_ref, a2_ref, g2_ref, w0_ref, a0_ref,
                    r_ref, k_ref, v_ref, wl_ref, a_ref, g_ref, shift_ref, hbuf):
    j = pl.program_id(1)
    tm = x_ref.shape[0]
    pad = V7X_SUBLANES
    h = _rms(x_ref[...], ln_ref[...])

    @pl.when(j == 0)
    def _():
        hbuf[pad - 1:pad, :] = hprev_ref[...]

    hbuf[pad:pad + tm, :] = h
    dx = hbuf[pad - 1:pad - 1 + tm, :] - h
    hbuf[pad - 1:pad, :] = h[tm - 1:tm, :]
    shift_ref[...] = h[tm - 1:tm, :]

    def mixed(i):
        return (h + dx * mix_ref[i:i + 1, :]).astype(BF16)

    r_ref[...] = jnp.dot(mixed(0), wr_ref[...], preferred_element_type=F32)
    k_ref[...] = jnp.dot(mixed(2), wk_ref[...], preferred_element_type=F32)
    v_ref[...] = jnp.dot(mixed(3), wv_ref[...], preferred_element_type=F32)
    tw = jnp.tanh(jnp.dot(mixed(1), w1_ref[...], preferred_element_type=F32))
    wpre = w0_ref[...] + _mm(tw, w2_ref[...])
    sp = jnp.maximum(-wpre, 0.0) + jnp.log(1.0 + jnp.exp(-jnp.abs(wpre)))
    wl_ref[...] = -jnp.exp(-sp - 0.5)
    ta = jnp.dot(mixed(4), a1_ref[...], preferred_element_type=F32)
    a_ref[...] = jax.nn.sigmoid(a0_ref[...] + _mm(ta, a2_ref[...]))
    tg = jax.nn.sigmoid(jnp.dot(mixed(5), g1_ref[...], preferred_element_type=F32))
    g_ref[...] = _mm(tg, g2_ref[...])


def _rwkv_proj(x, hprev, ln, mix, wr, wk, wv, w1, a1, g1, w2, a2, g2, w0, a0, tm):
    nb, t, d = x.shape
    assert t % tm == 0
    tok = pl.BlockSpec((None, tm, d), lambda b, j: (b, j, 0))
    row = pl.BlockSpec((None, 1, d), lambda b, j: (b, 0, 0))
    ins = [x, hprev.reshape(nb, 1, d), ln, mix, wr, wk, wv, w1, a1, g1, w2, a2, g2, w0, a0]
    in_specs = [tok, row] + [_const_spec(a.shape) for a in ins[2:]]
    big = jax.ShapeDtypeStruct((nb, t, d), F32)
    outs = pl.pallas_call(
        _rwkv_proj_body,
        grid=(nb, t // tm),
        in_specs=in_specs,
        out_specs=[tok] * 6 + [row],
        out_shape=[big] * 6 + [jax.ShapeDtypeStruct((nb, 1, d), F32)],
        scratch_shapes=[pltpu.VMEM((tm + 2 * V7X_SUBLANES, d), F32)],
        compiler_params=_cparams(("arbitrary", "arbitrary")),
        name="rwkv_proj",
    )(*ins)
    return outs[:6], outs[6].reshape(nb, d)


_INV_BASE = 8


def _rwkv_chunks(r, k, v, wl, a, g, sbd, kkp, kap, rkp, lnw, lnb, m):
    c = r[0].shape[0]
    pairs = len(sbd)
    n_chunks = len(r) // pairs
    kkp, kap, rkp, lnw, lnb = (x * n_chunks for x in (kkp, kap, rkp, lnw, lnb))
    ch0, th0, strict, incl, eye_cat, tril_bf, bd128, n_square, rowb, colb, levels = m
    zero = jnp.zeros((), F32)
    each = lambda fn, *ls: [fn(*xs) for xs in zip(*ls)]

    def seg_sum(x):
        s0 = jnp.sum(jnp.where(ch0, x, zero), axis=-1, keepdims=True)
        s1 = jnp.sum(jnp.where(ch0, zero, x), axis=-1, keepdims=True)
        return jnp.where(ch0, s0, s1)

    def bd_chan(x):
        return jnp.concatenate([jnp.where(ch0, x, zero), jnp.where(ch0, zero, x)], axis=0)

    def bd_chan_swapped(x):
        return jnp.concatenate([jnp.where(ch0, zero, x), jnp.where(ch0, x, zero)], axis=0)

    def bd_time(x):
        return jnp.concatenate([jnp.where(th0, x, zero), jnp.where(th0, zero, x)], axis=0)

    def normalised(k_, kkp_):
        kk_ = k_ * kkp_
        return kk_ * lax.rsqrt(jnp.maximum(seg_sum(kk_ * kk_), KK_EPS))

    def cumulative(wl_):
        w_hi = wl_.astype(BF16)
        w_r1 = wl_ - w_hi.astype(F32)
        w_mid = w_r1.astype(BF16)
        w_lo = (w_r1 - w_mid.astype(F32)).astype(BF16)
        return (jnp.dot(tril_bf, w_hi, preferred_element_type=F32)
                + (jnp.dot(tril_bf, w_mid, preferred_element_type=F32)
                   + jnp.dot(tril_bf, w_lo, preferred_element_type=F32)))

    kk = each(normalised, k, kkp)
    k2 = each(lambda k_, a_, kap_: k_ * (1.0 + (a_ - 1.0) * kap_), k, a, kap)
    b = each(lambda kk_, a_: kk_ * a_, kk, a)
    cum = each(cumulative, wl)
    cum_end = [x[c - 1:c, :] for x in cum]
    lhs = each(lambda kk_, r_, cum_, wl_: jnp.concatenate(
        [kk_ * jnp.exp(cum_ - wl_), r_ * jnp.exp(cum_)], axis=0), kk, r, cum, wl)
    inv_w = [jnp.exp(-x) for x in cum]
    bt = each(lambda b_, w_: b_ * w_, b, inv_w)
    kt = each(lambda k_, w_: k_ * w_, k2, inv_w)
    to_end = each(lambda e_, cum_: jnp.exp(e_ - cum_), cum_end, cum)
    rhs_end = each(lambda b_, k_, w_: jnp.concatenate([b_ * w_, k_ * w_], axis=0), b, k2, to_end)

    g0 = each(lambda l_, bt_, kt_: _mm(jnp.where(ch0, l_, zero), jnp.concatenate([bt_, kt_], axis=0), _NT),
              lhs, bt, kt)
    g1 = each(lambda l_, bt_, kt_: _mm(jnp.where(ch0, zero, l_), jnp.concatenate([kt_, bt_], axis=0), _NT),
              lhs, bt, kt)
    mab_c = each(lambda x, y: jnp.where(strict, jnp.where(th0, x[:c], y[:c]), zero), g0, g1)
    mak_s = each(lambda x, y: jnp.where(strict, jnp.where(th0, y[:c], x[:c]), zero), g0, g1)
    nab_c = each(lambda x, y: jnp.where(incl, jnp.where(th0, x[c:], y[c:]), zero), g0, g1)
    nak_s = each(lambda x, y: jnp.where(incl, jnp.where(th0, y[c:], x[c:]), zero), g0, g1)

    mn = each(lambda m_, n_, v_: _mm(jnp.concatenate([m_, n_], axis=0), bd_chan_swapped(v_)),
              mak_s, nak_s, v)

    xp = [jnp.where(rowb == colb, -x, zero) for x in mab_c]
    p = [eye_cat + x for x in xp]
    for _ in range(n_square):
        xp = each(lambda x: _mm(x, bd_time(x)), xp)
        p = each(lambda p_, x: p_ + _mm(p_, bd_time(x)), p, xp)
    for lev in levels:
        half = each(lambda p_, m_: _mm(p_, bd_time(jnp.where(lev, m_, zero))), p, mab_c)
        p = each(lambda p_, h_: p_ - _mm(h_, bd_time(p_)), p, half)

    y = []
    for j in range(n_chunks):
        it = slice(j * pairs, (j + 1) * pairs)
        ls = each(lambda l_, s_: _mm(l_, s_, _NT), lhs[it], sbd)
        sa = each(lambda p_, ls_, mn_: -_mm(p_, bd_chan(ls_[:c] + mn_[:c])), p[it], ls, mn[it])
        upd = each(lambda sa_, v_, e_: _mm(jnp.concatenate([sa_, v_], axis=0), e_, _TN),
                   sa, v[it], rhs_end[it])
        sbd = each(lambda s_, e_, u_: s_ * jnp.exp(e_) + jnp.where(bd128, u_, zero),
                   sbd, cum_end[it], upd)
        y += each(lambda ls_, n_, sa_, mn_: ls_[c:] + _mm(n_, bd_chan(sa_)) + mn_[c:],
                  ls, nab_c[it], sa, mn[it])
    sbd_new = sbd

    inv_n = 1.0 / HEAD_DIM

    def finish(y_, r_, k2_, v_, g_, rkp_, lnw_, lnb_):
        yc = y_ - seg_sum(y_) * inv_n
        var = seg_sum(yc * yc) * inv_n
        yn = yc * lax.rsqrt(var + GN_EPS) * lnw_ + lnb_
        return (yn + seg_sum(r_ * k2_ * rkp_) * v_) * g_

    return each(finish, y, r, k2, v, g, rkp, lnw, lnb), sbd_new


def _rwkv_scan_body(r_ref, k_ref, v_ref, wl_ref, a_ref, g_ref, s0_ref,
                    kkp_ref, kap_ref, rkp_ref, lnw_ref, lnb_ref,
                    z_ref, sout_ref, s_scr, *, chunk):
    ci = pl.program_id(2)
    c = chunk
    n_chunks = r_ref.shape[0] // c
    n = HEAD_DIM
    w = 2 * n
    pairs = r_ref.shape[1] // w
    zpad = jnp.zeros((n, n), F32)

    @pl.when(ci == 0)
    def _():
        for pi in range(pairs):
            top = jnp.concatenate([s0_ref[2 * pi], zpad], axis=1)
            bot = jnp.concatenate([zpad, s0_ref[2 * pi + 1]], axis=1)
            s_scr[pi] = jnp.concatenate([top, bot], axis=0)

    lane = lax.broadcasted_iota(jnp.int32, (1, w), 1)
    rowt = lax.broadcasted_iota(jnp.int32, (c, 2 * c), 0)
    colt = lax.broadcasted_iota(jnp.int32, (c, 2 * c), 1)
    th0 = colt < c
    tcol = jnp.where(th0, colt, colt - c)
    row_c = lax.broadcasted_iota(jnp.int32, (c, c), 0)
    col_c = lax.broadcasted_iota(jnp.int32, (c, c), 1)
    r128 = lax.broadcasted_iota(jnp.int32, (w, w), 0)
    c128 = lax.broadcasted_iota(jnp.int32, (w, w), 1)
    base = min(c, _INV_BASE)
    base_log = int(math.log2(base))
    assert 1 << base_log == base and c % base == 0 and (c // base) & (c // base - 1) == 0
    levels = []
    size_log = base_log
    while (1 << size_log) < c:
        rb = lax.shift_right_logical(rowt, size_log)
        cb = lax.shift_right_logical(tcol, size_log)
        levels.append(((rb & 1) == 1) & (cb == rb - 1))
        size_log += 1
    masks = (lane < n, th0, tcol < rowt, tcol <= rowt, jnp.where(tcol == rowt, 1.0, 0.0).astype(F32),
             jnp.where(col_c <= row_c, 1.0, 0.0).astype(BF16), (r128 < n) == (c128 < n),
             base_log - 1, lax.shift_right_logical(rowt, base_log),
             lax.shift_right_logical(tcol, base_log), levels)
    lanes = [slice(pi * w, (pi + 1) * w) for pi in range(pairs)]
    rows = [slice(j * c, (j + 1) * c) for j in range(n_chunks)]
    per_item = lambda ref: [ref[rw, sl] for rw in rows for sl in lanes]
    per_pair = lambda ref: [ref[:, sl] for sl in lanes]
    z, s_new = _rwkv_chunks(
        per_item(r_ref), per_item(k_ref), per_item(v_ref), per_item(wl_ref), per_item(a_ref),
        per_item(g_ref), [s_scr[pi] for pi in range(pairs)], per_pair(kkp_ref), per_pair(kap_ref),
        per_pair(rkp_ref), per_pair(lnw_ref), per_pair(lnb_ref), masks)
    for j in range(n_chunks):
        for pi in range(pairs):
            z_ref[rows[j], lanes[pi]] = z[j * pairs + pi]
    for pi in range(pairs):
        s_scr[pi] = s_new[pi]

    @pl.when(ci == pl.num_programs(2) - 1)
    def _():
        for pi in range(pairs):
            s = s_scr[pi]
            sout_ref[2 * pi] = s[:n, :n]
            sout_ref[2 * pi + 1] = s[n:, n:]


def _rwkv_scan(r, k, v, wl, a, g, s0, kkp, kap, rkp, lnw, lnb, chunk, chunks_per_step, heads_per_step):
    nb, t, d = r.shape
    n = HEAD_DIM
    nh = d // n
    gw = heads_per_step * n
    rows = chunk * chunks_per_step
    assert t % rows == 0 and nh % heads_per_step == 0 and gw % V7X_LANES == 0
    tok = pl.BlockSpec((None, rows, gw), lambda b, hg, ci: (b, ci, hg))
    st = pl.BlockSpec((None, heads_per_step, n, n), lambda b, hg, ci: (b, hg, 0, 0))
    par = pl.BlockSpec((1, gw), lambda b, hg, ci: (0, hg))
    z, s_out = pl.pallas_call(
        functools.partial(_rwkv_scan_body, chunk=chunk),
        grid=(nb, nh // heads_per_step, t // rows),
        in_specs=[tok] * 6 + [st] + [par] * 5,
        out_specs=[tok, st],
        out_shape=[jax.ShapeDtypeStruct((nb, t, d), F32),
                   jax.ShapeDtypeStruct((nb, nh, n, n), F32)],
        scratch_shapes=[pltpu.VMEM((heads_per_step // 2, 2 * n, 2 * n), F32)],
        compiler_params=_cparams(("arbitrary", "arbitrary", "arbitrary")),
        name="rwkv_scan",
    )(r, k, v, wl, a, g, s0, kkp, kap, rkp, lnw, lnb)
    return z, s_out


def _out_mlp_body(x_ref, z_ref, wo_ref, ln2_ref, up_ref, dn_ref, lnf_ref, y_ref, *, ff_chunk, final_norm):
    yp = x_ref[...] + jnp.dot(z_ref[...].astype(BF16), wo_ref[...], preferred_element_type=F32)
    h = _rms(yp, ln2_ref[...]).astype(BF16)
    acc = yp
    for c0 in range(0, up_ref.shape[1], ff_chunk):
        u = jnp.dot(h, up_ref[:, c0:c0 + ff_chunk], preferred_element_type=F32)
        u = jnp.square(jnp.maximum(u, 0.0)).astype(BF16)
        acc = acc + jnp.dot(u, dn_ref[c0:c0 + ff_chunk, :], preferred_element_type=F32)
    y_ref[...] = _rms(acc, lnf_ref[...]) if final_norm else acc


def _out_mlp(x, z, wo, ln2, up, dn, lnf, tm, final_norm):
    m, d = x.shape
    assert m % tm == 0
    tok = pl.BlockSpec((tm, d), lambda i: (i, 0))
    ins = [x, z, wo, ln2, up, dn, lnf]
    return pl.pallas_call(
        functools.partial(_out_mlp_body, ff_chunk=1024, final_norm=final_norm),
        grid=(m // tm,),
        in_specs=[tok, tok] + [_const_spec(a.shape) for a in ins[2:]],
        out_specs=tok,
        out_shape=jax.ShapeDtypeStruct((m, d), F32),
        compiler_params=_cparams(("arbitrary",)),
        name="out_mlp",
    )(*ins)


def _qkv_body(x_ref, ln_ref, w_ref, q_ref, k_ref, v_ref, *t_refs):
    d = x_ref.shape[1]
    h = _rms(x_ref[...], ln_ref[...]).astype(BF16)
    q_ref[...] = jnp.dot(h, w_ref[:, 0:d], preferred_element_type=F32)
    k = jnp.dot(h, w_ref[:, d:2 * d], preferred_element_type=F32)
    v = jnp.dot(h, w_ref[:, 2 * d:3 * d], preferred_element_type=F32)
    k_ref[...] = k
    v_ref[...] = v
    if t_refs:
        kt_ref, vt_ref = t_refs
        kt_ref[...] = k.T
        vt_ref[...] = v.T


def _qkv(x, ln, w, tm, transposed):
    nb, s, d = x.shape
    assert s % tm == 0
    tok = pl.BlockSpec((None, tm, d), lambda b, j: (b, j, 0))
    tok_t = pl.BlockSpec((None, d, tm), lambda b, j: (b, 0, j))
    sds = jax.ShapeDtypeStruct((nb, s, d), F32)
    sds_t = jax.ShapeDtypeStruct((nb, d, s), F32)
    return pl.pallas_call(
        _qkv_body,
        grid=(nb, s // tm),
        in_specs=[tok, _const_spec(ln.shape), _const_spec(w.shape)],
        out_specs=[tok] * 3 + ([tok_t] * 2 if transposed else []),
        out_shape=[sds] * 3 + ([sds_t] * 2 if transposed else []),
        compiler_params=_cparams(("arbitrary", "arbitrary")),
        name="moba_qkv",
    )(x, ln, w)


def _bucket_starts():
    n = np.arange(MAX_DISTANCE, dtype=np.float32)
    max_exact = N_BUCKETS // 2
    nf = np.maximum(n, 1.0).astype(np.float32)
    large = max_exact + (np.log(nf / np.float32(max_exact)) / np.float32(math.log(MAX_DISTANCE / max_exact))
                         * np.float32(N_BUCKETS - max_exact)).astype(np.int32)
    bucket = np.where(n < max_exact, n.astype(np.int32), np.minimum(large, N_BUCKETS - 1))
    starts = [int(np.argmax(bucket >= b)) if (bucket >= b).any() else MAX_DISTANCE
              for b in range(N_BUCKETS)]
    starts[N_BUCKETS - 1] = min(starts[N_BUCKETS - 1], MAX_DISTANCE)
    return starts


_BUCKET_STARTS = _bucket_starts()


def _bias_of_distance(rel, rb_ref, head):
    bias = jnp.full(rel.shape, rb_ref[0, head], F32)
    for b in range(1, N_BUCKETS):
        bias = jnp.where(rel >= _BUCKET_STARTS[b], rb_ref[b, head], bias)
    return bias


def _moba_bias_body(rb_ref, o_ref):
    h = pl.program_id(0)
    blk = o_ref.shape[-1]
    kj = lax.broadcasted_iota(jnp.int32, (blk, blk), 0)
    qi = lax.broadcasted_iota(jnp.int32, (blk, blk), 1)
    rel = qi - kj
    own = _bias_of_distance(jnp.maximum(rel, 0), rb_ref, h)
    o_ref[0] = jnp.where(rel >= 0, own, NEG_BIG)
    o_ref[1] = _bias_of_distance(rel + blk, rb_ref, h)


def _moba_bias_tiles(rel_bias, blk):
    nh = rel_bias.shape[1]
    return pl.pallas_call(
        _moba_bias_body,
        grid=(nh,),
        in_specs=[pl.BlockSpec(memory_space=pltpu.SMEM)],
        out_specs=pl.BlockSpec((None, 2, blk, blk), lambda h: (h, 0, 0, 0)),
        out_shape=jax.ShapeDtypeStruct((nh, 2, blk, blk), F32),
        compiler_params=_cparams(("arbitrary",)),
        name="moba_bias",
    )(rel_bias)


_ONES_ROWS = 16


def _moba_prompt_body(rb_ref, q_ref, k_ref, v_ref, bias_ref, o_ref,
                      kmean_scr, kb_scr, vt_scr, sel_scr, m_scr, l_scr, acc_scr):
    i = pl.program_id(2)
    hp = pl.program_id(1)
    blk = q_ref.shape[0]
    n = HEAD_DIM
    nblk = k_ref.shape[0] // blk
    heads = q_ref.shape[1] // n
    scale = HEAD_DIM ** -0.5

    @pl.when(i == 0)
    def _():
        ones = jnp.ones((_ONES_ROWS, blk), BF16)
        for b in range(nblk):
            kb = k_ref[b * blk:(b + 1) * blk, :]
            kmean_scr[b:b + 1, :] = jnp.mean(kb, axis=0, keepdims=True)
            kb_scr[b * blk:(b + 1) * blk, :] = kb.astype(BF16)
            vt = v_ref[b * blk:(b + 1) * blk, :].T.astype(BF16)
            for hh in range(heads):
                vt_scr[b, hh] = jnp.concatenate([vt[hh * n:(hh + 1) * n, :], ones], axis=0)

    lane = lax.broadcasted_iota(jnp.int32, (1, heads * n), 1)
    q = q_ref[...]
    blk_id = lax.broadcasted_iota(jnp.int32, (nblk, blk), 0)

    hs = range(heads)

    def attend(blocks, first):
        ks, vts, biases, sels = [], [], [], []
        for b, bias, sel_row in blocks:
            if sel_row is not None and bias[0].ndim == 0:
                bias = [jnp.where(sel_row[hh] > 0, bias[hh], NEG_BIG) for hh in hs]
                sel_row = None
            ks.append(kb_scr[pl.ds(pl.multiple_of(b * blk, blk), blk), :])
            vts.append(vt_scr[b])
            biases.append(bias)
            sels.append(sel_row)
        items = [(bi, hh) for bi in range(len(blocks)) for hh in hs]
        s = [lax.dot_general(ks[bi], qm[hh], _NT, preferred_element_type=F32) + biases[bi][hh]
             for bi, hh in items]
        s = [x if sels[bi] is None else jnp.where(sels[bi][hh] > 0, x, NEG_BIG)
             for x, (bi, hh) in zip(s, items)]
        m_blk = [jnp.max(x, axis=0, keepdims=True) for x in s]
        m_new = []
        for hh in hs:
            m = functools.reduce(jnp.maximum, [m_blk[bi * heads + hh] for bi in range(len(blocks))])
            m_new.append(m if first else jnp.maximum(m_scr[hh], m))
        p = [jnp.exp((x - m_new[hh]).astype(BF16)) for x, (bi, hh) in zip(s, items)]
        pv = [jnp.dot(vts[bi][hh], x, preferred_element_type=F32) for x, (bi, hh) in zip(p, items)]
        for hh in hs:
            tot = functools.reduce(lambda x, y: x + y, [pv[bi * heads + hh] for bi in range(len(blocks))])
            if first:
                l_scr[hh] = tot[n:n + 1, :]
                acc_scr[hh] = tot[:n]
            else:
                alpha = jnp.exp(m_scr[hh] - m_new[hh])
                l_scr[hh] = l_scr[hh] * alpha + tot[n:n + 1, :]
                acc_scr[hh] = acc_scr[hh] * alpha + tot[:n]
            m_scr[hh] = m_new[hh]

    in_head = [(lane >= hh * n) & (lane < (hh + 1) * n) for hh in hs]
    qm = [(jnp.where(in_head[hh], q, 0.0) * scale).astype(BF16) for hh in hs]
    kmean = kmean_scr[...]
    kmean_heads = jnp.concatenate([jnp.where(in_head[hh], kmean, 0.0) for hh in hs], axis=0)
    gate_all = lax.dot_general(kmean_heads, q, _NT, precision=HI, preferred_element_type=F32)
    for hh in hs:
        gate = gate_all[hh * nblk:(hh + 1) * nblk, :]
        cnt = jnp.zeros((nblk, blk), jnp.int32)
        for mb in range(nblk - 1):
            gm = gate[mb:mb + 1, :]
            beats = (gm > gate) | ((gm == gate) & (mb < blk_id))
            cnt = cnt + jnp.where(beats, (mb < i).astype(jnp.int32), 0)
        sel = (blk_id < i) & (cnt < MOBA_TOP_K)
        sel_scr[hh] = jnp.where(sel, 1.0, 0.0)

    own = (i, [bias_ref[hh, 0] for hh in hs], None)

    def selected(b):
        return [sel_scr[hh, pl.ds(b, 1), :] for hh in hs]

    def far(b):
        return (b, [rb_ref[N_BUCKETS - 1, hp * heads + hh] for hh in hs], selected(b))

    n_far = jnp.maximum(i - 1, 0)
    attend([own, (n_far, [bias_ref[hh, 1] for hh in hs], selected(n_far))], True)

    def far_pair(j, carry):
        attend([far(2 * j), far(2 * j + 1)], False)
        return carry

    lax.fori_loop(0, lax.shift_right_logical(n_far, 1), far_pair, 0)

    @pl.when((n_far & 1) == 1)
    def _():
        attend([far(n_far - 1)], False)

    out_t = jnp.concatenate([acc_scr[hh] / l_scr[hh] for hh in range(heads)], axis=0)
    o_ref[...] = out_t.T


def _moba_prompt(q, k, v, bias_tiles, rel_bias, heads):
    nb, s, d = q.shape
    blk = MOBA_BLOCK
    n = HEAD_DIM
    gw = heads * n
    nblk = s // blk
    assert s % blk == 0 and gw % V7X_LANES == 0 and d % gw == 0
    qspec = pl.BlockSpec((None, blk, gw), lambda b, hp, i: (b, i, hp))
    kvspec = pl.BlockSpec((None, s, gw), lambda b, hp, i: (b, 0, hp))
    bspec = pl.BlockSpec((heads, 2, blk, blk), lambda b, hp, i: (hp, 0, 0, 0))
    return pl.pallas_call(
        _moba_prompt_body,
        grid=(nb, d // gw, nblk),
        in_specs=[pl.BlockSpec(memory_space=pltpu.SMEM), qspec, kvspec, kvspec, bspec],
        out_specs=qspec,
        out_shape=jax.ShapeDtypeStruct((nb, s, d), F32),
        scratch_shapes=[
            pltpu.VMEM((nblk, gw), F32),
            pltpu.VMEM((s, gw), BF16),
            pltpu.VMEM((nblk, heads, n + _ONES_ROWS, blk), BF16),
            pltpu.VMEM((heads, nblk, blk), F32),
            pltpu.VMEM((heads, 1, blk), F32),
            pltpu.VMEM((heads, 1, blk), F32),
            pltpu.VMEM((heads, n, blk), F32),
        ],
        compiler_params=_cparams(("arbitrary", "arbitrary", "arbitrary")),
        name="moba_prompt",
    )(rel_bias, q, k, v, bias_tiles)


_GATE_BLOCKS_PER_STEP = 8


def _moba_gate_body(pt_ref, qt_ref, *rest, ppb):
    top_ref, gate_scr = rest[-2:]
    page_refs = rest[:-2]
    step = pl.program_id(1)
    blocks_per_step = len(page_refs) // ppb
    nblocks = gate_scr.shape[0]
    rows = ppb * page_refs[0].shape[-1]
    for bi in range(blocks_per_step):
        ksum = page_refs[bi * ppb][...]
        for w in range(1, ppb):
            ksum = ksum + page_refs[bi * ppb + w][...]
        kmean = jnp.sum(ksum, axis=-1, keepdims=True) * (1.0 / rows)
        gate_scr[step * blocks_per_step + bi] = jnp.sum(qt_ref[...] * kmean, axis=1, keepdims=True)

    @pl.when(step == pl.num_programs(1) - 1)
    def _():
        g = gate_scr[...]
        idx = lax.broadcasted_iota(jnp.int32, g.shape, 0)
        for j in range(MOBA_TOP_K):
            mx = jnp.max(g, axis=0, keepdims=True)
            first = jnp.min(jnp.where(g == mx, idx, nblocks), axis=0, keepdims=True)
            top_ref[j] = first[0]
            g = jnp.where(idx == first, -jnp.inf, g)


def _moba_gate_topk(qt, cache_kt, layer, page_table, n_past_blocks):
    nb, nh, n, t = qt.shape
    page = cache_kt.shape[-1]
    ppb = MOBA_BLOCK // page
    pages_per_step = _tile(n_past_blocks, _GATE_BLOCKS_PER_STEP) * ppb

    def kspec(which):
        return pl.BlockSpec((None, None, nh, n, page),
                            lambda b, j, pt: (layer, pt[b, j * pages_per_step + which], 0, 0, 0))

    grid_spec = pltpu.PrefetchScalarGridSpec(
        num_scalar_prefetch=1,
        grid=(nb, n_past_blocks * ppb // pages_per_step),
        in_specs=[pl.BlockSpec((None, nh, n, t), lambda b, j, pt: (b, 0, 0, 0))]
        + [kspec(w) for w in range(pages_per_step)],
        out_specs=pl.BlockSpec((None, MOBA_TOP_K, nh, 1, t), lambda b, j, pt: (b, 0, 0, 0, 0)),
        scratch_shapes=[pltpu.VMEM((n_past_blocks, nh, 1, t), F32)],
    )
    return pl.pallas_call(
        functools.partial(_moba_gate_body, ppb=ppb),
        grid_spec=grid_spec,
        out_shape=jax.ShapeDtypeStruct((nb, MOBA_TOP_K, nh, 1, t), jnp.int32),
        compiler_params=_cparams(("arbitrary", "arbitrary")),
        name="moba_gate",
    )(page_table, qt, *([cache_kt] * pages_per_step))


def _moba_sample_body(top_ref, pt_ref, rb_ref, q_ref, kn_ref, vn_ref, ck_ref, cv_ref, o_ref,
                      kbuf, vbuf, sems, *, layer, n_pages, past_len, nh):
    b = pl.program_id(0)
    h = pl.program_id(1)
    t, n = q_ref.shape
    page = kbuf.shape[-1]
    ppb = MOBA_BLOCK // page
    n_sel = MOBA_TOP_K * ppb
    scale = HEAD_DIM ** -0.5

    step = b * nh + h
    cur = lax.rem(step, 2)

    def block_of(ti, j, bb=b, hh=h):
        return top_ref[((bb * MOBA_TOP_K + j) * nh + hh) * t + ti]

    def copies(bb, hh, half, ti, j, p):
        phys = pt_ref[bb * n_pages + block_of(ti, j, bb, hh) * ppb + p]
        slot = (ti * MOBA_TOP_K + j) * ppb + p
        return (pltpu.make_async_copy(ck_ref.at[layer, phys, hh], kbuf.at[half, slot], sems.at[half, 0]),
                pltpu.make_async_copy(cv_ref.at[layer, phys, hh], vbuf.at[half, slot], sems.at[half, 1]))

    every = [(ti, j, p) for ti in range(t) for j in range(MOBA_TOP_K) for p in range(ppb)]

    def start_all(bb, hh, half):
        for idx in every:
            for cp in copies(bb, hh, half, *idx):
                cp.start()

    @pl.when(step == 0)
    def _():
        start_all(b, h, cur)

    @pl.when(step + 1 < pl.num_programs(0) * nh)
    def _():
        wraps = h == nh - 1
        start_all(jnp.where(wraps, b + 1, b), jnp.where(wraps, 0, h + 1), 1 - cur)

    for idx in every:
        for cp in copies(b, h, cur, *idx):
            cp.wait()

    qs = (q_ref[...] * scale).astype(BF16)
    n_slots = t * n_sel
    n_keys = n_slots * page
    seg = n_sel * page
    kt_all = jnp.concatenate([kbuf[cur, sl] for sl in range(n_slots)], axis=1)
    vt_all = jnp.concatenate([vbuf[cur, sl] for sl in range(n_slots)], axis=1)
    s_all = _mm(qs, kt_all)
    row = lax.broadcasted_iota(jnp.int32, (t, seg), 0)
    offs = lax.broadcasted_iota(jnp.int32, (1, MOBA_BLOCK), 1)
    s = jnp.zeros((t, seg), F32)
    kpos = jnp.zeros((t, seg), jnp.int32)
    for ti in range(t):
        s = jnp.where(row == ti, s_all[:, ti * seg:(ti + 1) * seg], s)
        pos = jnp.concatenate([block_of(ti, j) * MOBA_BLOCK + offs for j in range(MOBA_TOP_K)], axis=1)
        kpos = jnp.where(row == ti, pos, kpos)
    rel = (past_len + row) - kpos
    s = s + _bias_of_distance(jnp.maximum(rel, 0), rb_ref, h)
    r_o = lax.broadcasted_iota(jnp.int32, (t, t), 0)
    c_o = lax.broadcasted_iota(jnp.int32, (t, t), 1)
    rel_o = r_o - c_o
    s_own = _mm(qs, kn_ref[...], _NT) + _bias_of_distance(jnp.maximum(rel_o, 0), rb_ref, h)
    s_own = jnp.where(rel_o >= 0, s_own, NEG_BIG)
    m = jnp.maximum(jnp.max(s, axis=-1, keepdims=True), jnp.max(s_own, axis=-1, keepdims=True))
    p = jnp.exp(s - m)
    p_own = jnp.exp(s_own - m)
    denom = jnp.sum(p, axis=-1, keepdims=True) + jnp.sum(p_own, axis=-1, keepdims=True)
    p_all = jnp.concatenate([jnp.where(row == ti, p, 0.0) for ti in range(t)], axis=1)
    o_ref[...] = (_mm(p_all, vt_all, _NT) + _mm(p_own, vn_ref[...])) / denom


def _moba_sample(q, k_new, v_new, top, page_table, rel_bias, cache_kt, cache_vt, layer, past_len):
    nb, nh, t, n = q.shape
    page = cache_kt.shape[-1]
    n_pages = page_table.shape[1]
    ppb = MOBA_BLOCK // page
    tok = pl.BlockSpec((None, None, t, n), lambda b, h, *_: (b, h, 0, 0))
    any_spec = pl.BlockSpec(memory_space=pl.ANY)
    grid_spec = pltpu.PrefetchScalarGridSpec(
        num_scalar_prefetch=2,
        grid=(nb, nh),
        in_specs=[pl.BlockSpec(memory_space=pltpu.SMEM), tok, tok, tok, any_spec, any_spec],
        out_specs=tok,
        scratch_shapes=[
            pltpu.VMEM((2, t * MOBA_TOP_K * ppb, n, page), F32),
            pltpu.VMEM((2, t * MOBA_TOP_K * ppb, n, page), F32),
            pltpu.SemaphoreType.DMA((2, 2)),
        ],
    )
    return pl.pallas_call(
        functools.partial(_moba_sample_body, layer=layer, n_pages=n_pages, past_len=past_len, nh=nh),
        grid_spec=grid_spec,
        out_shape=jax.ShapeDtypeStruct((nb, nh, t, n), F32),
        compiler_params=_cparams(("arbitrary", "arbitrary")),
        name="moba_sample",
    )(top.reshape(-1), page_table.reshape(-1), rel_bias, q, k_new, v_new, cache_kt, cache_vt)


def _tile(m, pref):
    return pref if m % pref == 0 else m


def kernel(x_prompt, x_sample, state_wkv, state_shift, cache_k, cache_v, page_table, ln1, ln2, ln_f, mlp_up, mlp_down, rwkv_mix, rwkv_w_r, rwkv_w_k, rwkv_w_v, rwkv_w_o, rwkv_w0, rwkv_w1, rwkv_w2, rwkv_a0, rwkv_a1, rwkv_a2, rwkv_g1, rwkv_g2, rwkv_k_k, rwkv_k_a, rwkv_r_k, rwkv_ln_w, rwkv_ln_b, moba_w_qkv, moba_w_o, rel_bias):
    nb, s, d = x_prompt.shape
    nbd, t, _ = x_sample.shape
    n = HEAD_DIM
    nh = d // n
    depth = ln1.shape[0]
    assert depth == 2 and rwkv_mix.shape[0] == 1 and moba_w_qkv.shape[0] == 1
    page = cache_k.shape[2]
    past_len = page_table.shape[1] * page
    assert past_len % MOBA_BLOCK == 0 and MOBA_BLOCK % page == 0
    n_past_blocks = past_len // MOBA_BLOCK
    assert n_past_blocks >= MOBA_TOP_K and s % MOBA_BLOCK == 0

    row = lambda a: a.reshape(1, -1)
    bf = lambda a: a.astype(BF16)
    lnf = row(ln_f)

    proj_w = (row(ln1[0]), rwkv_mix[0], bf(rwkv_w_r[0]), bf(rwkv_w_k[0]), bf(rwkv_w_v[0]),
              bf(rwkv_w1[0]), bf(rwkv_a1[0]), bf(rwkv_g1[0]), bf(rwkv_w2[0]), bf(rwkv_a2[0]),
              bf(rwkv_g2[0]), row(rwkv_w0[0]), row(rwkv_a0[0]))
    scan_w = (row(rwkv_k_k[0]), row(rwkv_k_a[0]), row(rwkv_r_k[0]), row(rwkv_ln_w[0]), row(rwkv_ln_b[0]))
    mlp0 = (bf(rwkv_w_o[0]), row(ln2[0]), bf(mlp_up[0]), bf(mlp_down[0]), lnf)
    mlp1 = (bf(moba_w_o[0]), row(ln2[1]), bf(mlp_up[1]), bf(mlp_down[1]), lnf)

    def rwkv_layer(x, hprev, s0, tm, chunk, tm_mlp):
        streams, shift = _rwkv_proj(x, hprev, *proj_w, tm=tm)
        z, s_out = _rwkv_scan(*streams, s0, *scan_w, chunk=chunk,
                              chunks_per_step=_tile(x.shape[1] // chunk, 4), heads_per_step=_tile(nh, 16))
        y = _out_mlp(x.reshape(-1, d), z.reshape(-1, d), *mlp0, tm=tm_mlp, final_norm=False)
        return y, shift, s_out

    yp, shift_p, wkv_p = rwkv_layer(x_prompt, jnp.zeros((nb, d), F32), jnp.zeros((nb, nh, n, n), F32),
                                    _tile(s, 256), _tile(s, 64), _tile(nb * s, 512))
    ys, shift_s, wkv_s = rwkv_layer(x_sample, state_shift[0], state_wkv[0], t, t, nbd * t)

    w_qkv = bf(moba_w_qkv[0])
    qp, kp, vp, kp_t, vp_t = _qkv(yp.reshape(nb, s, d), row(ln1[1]), w_qkv, _tile(s, 512), True)
    qs, ks, vs = _qkv(ys.reshape(1, nbd * t, d), row(ln1[1]), w_qkv, nbd * t, False)
    bias_tiles = _moba_bias_tiles(rel_bias, MOBA_BLOCK)
    attn_p = _moba_prompt(qp, kp, vp, bias_tiles, rel_bias, heads=_tile(nh, 4))
    per_head = lambda a: a.reshape(nbd, t, nh, n)
    cache_kt = cache_k.transpose(0, 1, 3, 4, 2)
    cache_vt = cache_v.transpose(0, 1, 3, 4, 2)
    top = _moba_gate_topk(per_head(qs).transpose(0, 2, 3, 1), cache_kt, 0, page_table, n_past_blocks)
    heads_first = lambda a: per_head(a).transpose(0, 2, 1, 3)
    attn_s = _moba_sample(heads_first(qs), heads_first(ks), heads_first(vs), top, page_table,
                          rel_bias, cache_kt, cache_vt, 0, past_len)
    attn_s = attn_s.transpose(0, 2, 1, 3).reshape(nbd * t, d)
    y_prompt = _out_mlp(yp, attn_p.reshape(-1, d), *mlp1, tm=_tile(nb * s, 512), final_norm=True)
    y_sample = _out_mlp(ys, attn_s, *mlp1, tm=nbd * t, final_norm=True)

    seq_last = lambda a: a.reshape(1, nb, nh, n, s).transpose(0, 1, 4, 2, 3)
    return (y_prompt.reshape(nb, s, d), y_sample.reshape(nbd, t, d),
            wkv_p[None], shift_p[None], seq_last(kp_t), seq_last(vp_t),
            wkv_s[None], shift_s[None],
            ks.reshape(1, nbd, t, nh, n), vs.reshape(1, nbd, t, nh, n))
```

```python
import functools
import math

import numpy as np
import jax
import jax.numpy as jnp
from jax import lax
from jax.experimental import pallas as pl
from jax.experimental.pallas import tpu as pltpu

F32 = jnp.float32
BF16 = jnp.bfloat16

HEAD_DIM = 64
GN_EPS = 64e-5
RMS_EPS = 1e-6
NEG_BIG = -1e30
MOBA_BLOCK = 256
MOBA_TOP_K = 3
N_BUCKETS = 32
MAX_DISTANCE = 128
KK_EPS = 1e-24

V7X_LANES = 128
V7X_SUBLANES = 8
V7X_VMEM_LIMIT_BYTES = 52 * 1024 * 1024

HI = lax.Precision.HIGHEST
_NN = (((1,), (0,)), ((), ()))
_NT = (((1,), (1,)), ((), ()))
_TN = (((0,), (0,)), ((), ()))


def _cparams(sem):
    return pltpu.CompilerParams(dimension_semantics=sem, vmem_limit_bytes=V7X_VMEM_LIMIT_BYTES)


def _const_spec(shape):
    nd = len(shape)
    return pl.BlockSpec(shape, lambda *_: (0,) * nd)


def _rms(x, g):
    return x * lax.rsqrt(jnp.mean(x * x, axis=-1, keepdims=True) + RMS_EPS) * g


def _mm(a, b, dims=_NN):
    return lax.dot_general(a.astype(BF16), b.astype(BF16), dims, preferred_element_type=F32)


def _rwkv_proj_body(x_ref, hprev_ref, ln_ref, mix_ref, wr_ref, wk_ref, wv_ref,
                    w1_ref, a1_ref, g1_ref, w2_ref, a2_ref, g2_ref, w0_ref, a0_ref,
                    r_ref, k_ref, v_ref, wl_ref, a_ref, g_ref, shift_ref, hbuf):
    j = pl.program_id(1)
    tm = x_ref.shape[0]
    pad = V7X_SUBLANES
    h = _rms(x_ref[...], ln_ref[...])

    @pl.when(j == 0)
    def _():
        hbuf[pad - 1:pad, :] = hprev_ref[...]

    hbuf[pad:pad + tm, :] = h
    dx = hbuf[pad - 1:pad - 1 + tm, :] - h
    hbuf[pad - 1:pad, :] = h[tm - 1:tm, :]
    shift_ref[...] = h[tm - 1:tm, :]

    def mixed(i):
        return (h + dx * mix_ref[i:i + 1, :]).astype(BF16)

    r_ref[...] = jnp.dot(mixed(0), wr_ref[...], preferred_element_type=F32)
    k_ref[...] = jnp.dot(mixed(2), wk_ref[...], preferred_element_type=F32)
    v_ref[...] = jnp.dot(mixed(3), wv_ref[...], preferred_element_type=F32)
    tw = jnp.tanh(jnp.dot(mixed(1), w1_ref[...], preferred_element_type=F32))
    wpre = w0_ref[...] + _mm(tw, w2_ref[...])
    wl_ref[...] = -math.exp(-0.5) * jax.nn.sigmoid(wpre)
    ta = jnp.dot(mixed(4), a1_ref[...], preferred_element_type=F32)
    a_ref[...] = jax.nn.sigmoid(a0_ref[...] + _mm(ta, a2_ref[...]))
    tg = jax.nn.sigmoid(jnp.dot(mixed(5), g1_ref[...], preferred_element_type=F32))
    g_ref[...] = _mm(tg, g2_ref[...])


def _rwkv_proj(x, hprev, ln, mix, wr, wk, wv, w1, a1, g1, w2, a2, g2, w0, a0, tm):
    nb, t, d = x.shape
    assert t % tm == 0
    tok = pl.BlockSpec((None, tm, d), lambda b, j: (b, j, 0))
    row = pl.BlockSpec((None, 1, d), lambda b, j: (b, 0, 0))
    ins = [x, hprev.reshape(nb, 1, d), ln, mix, wr, wk, wv, w1, a1, g1, w2, a2, g2, w0, a0]
    in_specs = [tok, row] + [_const_spec(a.shape) for a in ins[2:]]
    big = jax.ShapeDtypeStruct((nb, t, d), F32)
    outs = pl.pallas_call(
        _rwkv_proj_body,
        grid=(nb, t // tm),
        in_specs=in_specs,
        out_specs=[tok] * 6 + [row],
        out_shape=[big] * 6 + [jax.ShapeDtypeStruct((nb, 1, d), F32)],
        scratch_shapes=[pltpu.VMEM((tm + 2 * V7X_SUBLANES, d), F32)],
        compiler_params=_cparams(("arbitrary", "arbitrary")),
        name="rwkv_proj",
    )(*ins)
    return outs[:6], outs[6].reshape(nb, d)


_INV_BASE = 8

def _rwkv_chunks(r, k, v, wl, a, g, sbd, kkp, kap, rkp, lnw, lnb, m):
    c, lanes = r[0].shape
    pairs = len(sbd)
    n_chunks = len(r) // pairs
    kkp, kap, rkp, lnw, lnb = (x * n_chunks for x in (kkp, kap, rkp, lnw, lnb))
    ch0, th0, strict, incl, eye_cat, tril_bf, bd128, n_square, rowb, colb, levels = m
    zero = jnp.zeros((), F32)
    each = lambda fn, *ls: [fn(*xs) for xs in zip(*ls)]

    def seg_sum(x):
        s0 = jnp.sum(jnp.where(ch0, x, zero), axis=-1, keepdims=True)
        s1 = jnp.sum(jnp.where(ch0, zero, x), axis=-1, keepdims=True)
        return jnp.where(ch0, s0, s1)

    def bd_chan(x):
        return jnp.concatenate([jnp.where(ch0, x, zero), jnp.where(ch0, zero, x)], axis=0)

    def bd_chan_swapped(x):
        return jnp.concatenate([jnp.where(ch0, zero, x), jnp.where(ch0, x, zero)], axis=0)

    def bd_time(x):
        return jnp.concatenate([jnp.where(th0, x, zero), jnp.where(th0, zero, x)], axis=0)

    def normalised(k_, kkp_):
        kk_ = k_ * kkp_
        return kk_ * lax.rsqrt(jnp.maximum(seg_sum(kk_ * kk_), KK_EPS))

    def cumulative(wl_):
        w_hi = wl_.astype(BF16)
        w_r1 = wl_ - w_hi.astype(F32)
        w_mid = w_r1.astype(BF16)
        w_lo = (w_r1 - w_mid.astype(F32)).astype(BF16)
        return (jnp.dot(tril_bf, w_hi, preferred_element_type=F32)
                + (jnp.dot(tril_bf, w_mid, preferred_element_type=F32)
                   + jnp.dot(tril_bf, w_lo, preferred_element_type=F32)))

    def independent(r, k, v, wl, a, kkp, kap):
        kk = each(normalised, k, kkp)
        k2 = each(lambda k_, a_, kap_: k_ * (1.0 + (a_ - 1.0) * kap_), k, a, kap)
        b = each(lambda kk_, a_: kk_ * a_, kk, a)
        cum = []
        for j in range(n_chunks):
            full = cumulative(jnp.concatenate(wl[j * pairs:(j + 1) * pairs], axis=1))
            cum += [full[:, pi * lanes:(pi + 1) * lanes] for pi in range(pairs)]
        cum_end = [x[c - 1:c, :] for x in cum]
        lhs = each(lambda kk_, r_, cum_, wl_: jnp.concatenate(
            [kk_ * jnp.exp(cum_ - wl_), r_ * jnp.exp(cum_)], axis=0), kk, r, cum, wl)
        inv_w = [jnp.exp(-x) for x in cum]
        bt = each(lambda b_, w_: b_ * w_, b, inv_w)
        kt = each(lambda k_, w_: k_ * w_, k2, inv_w)
        to_end = each(lambda e_, cum_: jnp.exp(e_ - cum_), cum_end, cum)
        rhs_end = each(lambda b_, k_, w_: jnp.concatenate([b_ * w_, k_ * w_], axis=0), b, k2, to_end)

        if (2 * c) % V7X_LANES == 0:
            g01 = each(lambda l_, bt_, kt_: _mm(l_, jnp.concatenate(
                [jnp.where(ch0, bt_, zero), jnp.where(ch0, kt_, zero),
                 jnp.where(ch0, zero, kt_), jnp.where(ch0, zero, bt_)], axis=0), _NT), lhs, bt, kt)
            g0 = [x[:, :2 * c] for x in g01]
            g1 = [x[:, 2 * c:] for x in g01]
        else:
            g0 = each(lambda l_, bt_, kt_: _mm(jnp.where(ch0, l_, zero),
                                               jnp.concatenate([bt_, kt_], axis=0), _NT), lhs, bt, kt)
            g1 = each(lambda l_, bt_, kt_: _mm(jnp.where(ch0, zero, l_),
                                               jnp.concatenate([kt_, bt_], axis=0), _NT), lhs, bt, kt)
        mab_c = each(lambda x, y: jnp.where(strict, jnp.where(th0, x[:c], y[:c]), zero), g0, g1)
        mak_s = each(lambda x, y: jnp.where(strict, jnp.where(th0, y[:c], x[:c]), zero), g0, g1)
        nab_c = each(lambda x, y: jnp.where(incl, jnp.where(th0, x[c:], y[c:]), zero), g0, g1)
        nak_s = each(lambda x, y: jnp.where(incl, jnp.where(th0, y[c:], x[c:]), zero), g0, g1)

        mn = each(lambda m_, n_, v_: _mm(jnp.concatenate([m_, n_], axis=0), bd_chan_swapped(v_)),
                  mak_s, nak_s, v)

        xp = [jnp.where(rowb == colb, -x, zero) for x in mab_c]
        p = [eye_cat + x for x in xp]
        for _ in range(n_square):
            xp = each(lambda x: _mm(x, bd_time(x)), xp)
            p = each(lambda p_, x: p_ + _mm(p_, bd_time(x)), p, xp)
        for lev in levels:
            half = each(lambda p_, m_: _mm(p_, bd_time(jnp.where(lev, m_, zero))), p, mab_c)
            p = each(lambda p_, h_: p_ - _mm(h_, bd_time(p_)), p, half)
        return k2, cum_end, lhs, rhs_end, nab_c, mn, p

    k2, cum_end, lhs, rhs_end, nab_c, mn, p = independent(r, k, v, wl, a, kkp, kap)

    y = []
    for j in range(n_chunks):
        it = slice(j * pairs, (j + 1) * pairs)
        ls = each(lambda l_, s_: _mm(l_, s_, _NT), lhs[it], sbd)
        sa = each(lambda p_, ls_, mn_: -_mm(p_, bd_chan(ls_[:c] + mn_[:c])), p[it], ls, mn[it])
        upd = each(lambda sa_, v_, e_: _mm(jnp.concatenate([sa_, v_], axis=0), e_, _TN),
                   sa, v[it], rhs_end[it])
        sbd = each(lambda s_, e_, u_: s_ * jnp.exp(e_) + jnp.where(bd128, u_, zero),
                   sbd, cum_end[it], upd)
        y += each(lambda ls_, n_, sa_, mn_: ls_[c:] + _mm(n_, bd_chan(sa_)) + mn_[c:],
                  ls, nab_c[it], sa, mn[it])
    sbd_new = sbd

    inv_n = 1.0 / HEAD_DIM

    def finish(y_, r_, k2_, v_, g_, rkp_, lnw_, lnb_):
        yc = y_ - seg_sum(y_) * inv_n
        var = seg_sum(yc * yc) * inv_n
        yn = yc * lax.rsqrt(var + GN_EPS) * lnw_ + lnb_
        return (yn + seg_sum(r_ * k2_ * rkp_) * v_) * g_

    return each(finish, y, r, k2, v, g, rkp, lnw, lnb), sbd_new


def _rwkv_scan_body(r_ref, k_ref, v_ref, wl_ref, a_ref, g_ref, s0_ref,
                    kkp_ref, kap_ref, rkp_ref, lnw_ref, lnb_ref,
                    z_ref, sout_ref, s_scr, *, chunk):
    ci = pl.program_id(2)
    c = chunk
    n_chunks = r_ref.shape[0] // c
    n = HEAD_DIM
    w = 2 * n
    pairs = r_ref.shape[1] // w
    zpad = jnp.zeros((n, n), F32)

    @pl.when(ci == 0)
    def _():
        for pi in range(pairs):
            top = jnp.concatenate([s0_ref[2 * pi], zpad], axis=1)
            bot = jnp.concatenate([zpad, s0_ref[2 * pi + 1]], axis=1)
            s_scr[pi] = jnp.concatenate([top, bot], axis=0)

    lane = lax.broadcasted_iota(jnp.int32, (1, w), 1)
    rowt = lax.broadcasted_iota(jnp.int32, (c, 2 * c), 0)
    colt = lax.broadcasted_iota(jnp.int32, (c, 2 * c), 1)
    th0 = colt < c
    tcol = jnp.where(th0, colt, colt - c)
    row_c = lax.broadcasted_iota(jnp.int32, (c, c), 0)
    col_c = lax.broadcasted_iota(jnp.int32, (c, c), 1)
    r128 = lax.broadcasted_iota(jnp.int32, (w, w), 0)
    c128 = lax.broadcasted_iota(jnp.int32, (w, w), 1)
    base = min(c, _INV_BASE)
    base_log = int(math.log2(base))
    assert 1 << base_log == base and c % base == 0 and (c // base) & (c // base - 1) == 0
    levels = []
    size_log = base_log
    while (1 << size_log) < c:
        rb = lax.shift_right_logical(rowt, size_log)
        cb = lax.shift_right_logical(tcol, size_log)
        levels.append(((rb & 1) == 1) & (cb == rb - 1))
        size_log += 1
    masks = (lane < n, th0, tcol < rowt, tcol <= rowt, jnp.where(tcol == rowt, 1.0, 0.0).astype(F32),
             jnp.where(col_c <= row_c, 1.0, 0.0).astype(BF16), (r128 < n) == (c128 < n),
             base_log - 1, lax.shift_right_logical(rowt, base_log),
             lax.shift_right_logical(tcol, base_log), levels)
    lanes = [slice(pi * w, (pi + 1) * w) for pi in range(pairs)]
    rows = [slice(j * c, (j + 1) * c) for j in range(n_chunks)]
    per_item = lambda ref: [ref[rw, sl] for rw in rows for sl in lanes]
    per_pair = lambda ref: [ref[:, sl] for sl in lanes]
    z, s_new = _rwkv_chunks(
        per_item(r_ref), per_item(k_ref), per_item(v_ref), per_item(wl_ref), per_item(a_ref),
        per_item(g_ref), [s_scr[pi] for pi in range(pairs)], per_pair(kkp_ref), per_pair(kap_ref),
        per_pair(rkp_ref), per_pair(lnw_ref), per_pair(lnb_ref), masks)
    for j in range(n_chunks):
        for pi in range(pairs):
            z_ref[rows[j], lanes[pi]] = z[j * pairs + pi]
    for pi in range(pairs):
        s_scr[pi] = s_new[pi]

    @pl.when(ci == pl.num_programs(2) - 1)
    def _():
        for pi in range(pairs):
            s = s_scr[pi]
            sout_ref[2 * pi] = s[:n, :n]
            sout_ref[2 * pi + 1] = s[n:, n:]


def _rwkv_scan(r, k, v, wl, a, g, s0, kkp, kap, rkp, lnw, lnb, chunk, chunks_per_step, heads_per_step):
    nb, t, d = r.shape
    n = HEAD_DIM
    nh = d // n
    gw = heads_per_step * n
    rows = chunk * chunks_per_step
    assert t % rows == 0 and nh % heads_per_step == 0 and gw % V7X_LANES == 0
    tok = pl.BlockSpec((None, rows, gw), lambda b, hg, ci: (b, ci, hg))
    st = pl.BlockSpec((None, heads_per_step, n, n), lambda b, hg, ci: (b, hg, 0, 0))
    par = pl.BlockSpec((1, gw), lambda b, hg, ci: (0, hg))
    z, s_out = pl.pallas_call(
        functools.partial(_rwkv_scan_body, chunk=chunk),
        grid=(nb, nh // heads_per_step, t // rows),
        in_specs=[tok] * 6 + [st] + [par] * 5,
        out_specs=[tok, st],
        out_shape=[jax.ShapeDtypeStruct((nb, t, d), F32),
                   jax.ShapeDtypeStruct((nb, nh, n, n), F32)],
        scratch_shapes=[pltpu.VMEM((heads_per_step // 2, 2 * n, 2 * n), F32)],
        compiler_params=_cparams(("arbitrary", "arbitrary", "arbitrary")),
        name="rwkv_scan",
    )(r, k, v, wl, a, g, s0, kkp, kap, rkp, lnw, lnb)
    return z, s_out


def _out_mlp_body(x_ref, z_ref, wo_ref, ln2_ref, up_ref, dn_ref, lnf_ref, y_ref, *, ff_chunk, final_norm):
    yp = x_ref[...] + jnp.dot(z_ref[...].astype(BF16), wo_ref[...], preferred_element_type=F32)
    h = _rms(yp, ln2_ref[...]).astype(BF16)
    acc = yp
    for c0 in range(0, up_ref.shape[1], ff_chunk):
        u = jnp.dot(h, up_ref[:, c0:c0 + ff_chunk], preferred_element_type=F32)
        u = jnp.square(jnp.maximum(u, 0.0)).astype(BF16)
        acc = acc + jnp.dot(u, dn_ref[c0:c0 + ff_chunk, :], preferred_element_type=F32)
    y_ref[...] = _rms(acc, lnf_ref[...]) if final_norm else acc


def _out_mlp(x, z, wo, ln2, up, dn, lnf, tm, final_norm):
    m, d = x.shape
    assert m % tm == 0
    tok = pl.BlockSpec((tm, d), lambda i: (i, 0))
    ins = [x, z, wo, ln2, up, dn, lnf]
    return pl.pallas_call(
        functools.partial(_out_mlp_body, ff_chunk=1024, final_norm=final_norm),
        grid=(m // tm,),
        in_specs=[tok, tok] + [_const_spec(a.shape) for a in ins[2:]],
        out_specs=tok,
        out_shape=jax.ShapeDtypeStruct((m, d), F32),
        compiler_params=_cparams(("arbitrary",)),
        name="out_mlp",
    )(*ins)


def _qkv_body(x_ref, ln_ref, w_ref, q_ref, k_ref, v_ref, *t_refs):
    d = x_ref.shape[1]
    h = _rms(x_ref[...], ln_ref[...]).astype(BF16)
    q_ref[...] = jnp.dot(h, w_ref[:, 0:d], preferred_element_type=F32)
    k = jnp.dot(h, w_ref[:, d:2 * d], preferred_element_type=F32)
    v = jnp.dot(h, w_ref[:, 2 * d:3 * d], preferred_element_type=F32)
    k_ref[...] = k
    v_ref[...] = v
    if t_refs:
        kt_ref, vt_ref = t_refs
        kt_ref[...] = k.T
        vt_ref[...] = v.T


def _qkv(x, ln, w, tm, transposed):
    nb, s, d = x.shape
    assert s % tm == 0
    tok = pl.BlockSpec((None, tm, d), lambda b, j: (b, j, 0))
    tok_t = pl.BlockSpec((None, d, tm), lambda b, j: (b, 0, j))
    sds = jax.ShapeDtypeStruct((nb, s, d), F32)
    sds_t = jax.ShapeDtypeStruct((nb, d, s), F32)
    return pl.pallas_call(
        _qkv_body,
        grid=(nb, s // tm),
        in_specs=[tok, _const_spec(ln.shape), _const_spec(w.shape)],
        out_specs=[tok] * 3 + ([tok_t] * 2 if transposed else []),
        out_shape=[sds] * 3 + ([sds_t] * 2 if transposed else []),
        compiler_params=_cparams(("arbitrary", "arbitrary")),
        name="moba_qkv",
    )(x, ln, w)


def _bucket_starts():
    n = np.arange(MAX_DISTANCE, dtype=np.float32)
    max_exact = N_BUCKETS // 2
    nf = np.maximum(n, 1.0).astype(np.float32)
    large = max_exact + (np.log(nf / np.float32(max_exact)) / np.float32(math.log(MAX_DISTANCE / max_exact))
                         * np.float32(N_BUCKETS - max_exact)).astype(np.int32)
    bucket = np.where(n < max_exact, n.astype(np.int32), np.minimum(large, N_BUCKETS - 1))
    starts = [int(np.argmax(bucket >= b)) if (bucket >= b).any() else MAX_DISTANCE
              for b in range(N_BUCKETS)]
    starts[N_BUCKETS - 1] = min(starts[N_BUCKETS - 1], MAX_DISTANCE)
    return starts


_BUCKET_STARTS = _bucket_starts()


def _bias_of_distance(rel, rb_ref, head):
    bias = jnp.full(rel.shape, rb_ref[0, head], F32)
    for b in range(1, N_BUCKETS):
        bias = jnp.where(rel >= _BUCKET_STARTS[b], rb_ref[b, head], bias)
    return bias


def _moba_bias_body(rb_ref, o_ref):
    h = pl.program_id(0)
    blk = o_ref.shape[-1]
    kj = lax.broadcasted_iota(jnp.int32, (blk, blk), 0)
    qi = lax.broadcasted_iota(jnp.int32, (blk, blk), 1)
    rel = qi - kj
    own = _bias_of_distance(jnp.maximum(rel, 0), rb_ref, h)
    o_ref[0] = jnp.where(rel >= 0, own, NEG_BIG)
    o_ref[1] = _bias_of_distance(rel + blk, rb_ref, h)


def _moba_bias_tiles(rel_bias, blk):
    nh = rel_bias.shape[1]
    return pl.pallas_call(
        _moba_bias_body,
        grid=(nh,),
        in_specs=[pl.BlockSpec(memory_space=pltpu.SMEM)],
        out_specs=pl.BlockSpec((None, 2, blk, blk), lambda h: (h, 0, 0, 0)),
        out_shape=jax.ShapeDtypeStruct((nh, 2, blk, blk), F32),
        compiler_params=_cparams(("arbitrary",)),
        name="moba_bias",
    )(rel_bias)


_ONES_ROWS = 16


def _moba_prompt_body(rb_ref, q_ref, k_ref, v_ref, bias_ref, o_ref,
                      kmean_scr, kb_scr, vt_scr, sel_scr, m_scr, l_scr, acc_scr):
    i = pl.program_id(2)
    hp = pl.program_id(1)
    blk = q_ref.shape[0]
    n = HEAD_DIM
    nblk = k_ref.shape[0] // blk
    heads = q_ref.shape[1] // n
    scale = HEAD_DIM ** -0.5

    @pl.when(i == 0)
    def _():
        ones = jnp.ones((_ONES_ROWS, blk), BF16)
        for b in range(nblk):
            kb = k_ref[b * blk:(b + 1) * blk, :]
            kmean_scr[b:b + 1, :] = jnp.mean(kb, axis=0, keepdims=True)
            kb_scr[b * blk:(b + 1) * blk, :] = kb.astype(BF16)
            vt = v_ref[b * blk:(b + 1) * blk, :].T.astype(BF16)
            for hh in range(heads):
                vt_scr[b, hh] = jnp.concatenate([vt[hh * n:(hh + 1) * n, :], ones], axis=0)

    lane = lax.broadcasted_iota(jnp.int32, (1, heads * n), 1)
    q = q_ref[...]
    blk_id = lax.broadcasted_iota(jnp.int32, (nblk, blk), 0)

    hs = range(heads)

    def attend(blocks, first):
        ks, vts, biases, sels = [], [], [], []
        for b, bias, sel_row in blocks:
            if sel_row is not None and bias[0].ndim == 0:
                bias = [jnp.where(sel_row[hh] > 0, bias[hh], NEG_BIG) for hh in hs]
                sel_row = None
            ks.append(kb_scr[pl.ds(pl.multiple_of(b * blk, blk), blk), :])
            vts.append(vt_scr[b])
            biases.append(bias)
            sels.append(sel_row)
        items = [(bi, hh) for bi in range(len(blocks)) for hh in hs]
        s = [lax.dot_general(ks[bi], qm[hh], _NT, preferred_element_type=F32) + biases[bi][hh]
             for bi, hh in items]
        s = [x if sels[bi] is None else jnp.where(sels[bi][hh] > 0, x, NEG_BIG)
             for x, (bi, hh) in zip(s, items)]
        m_blk = [jnp.max(x, axis=0, keepdims=True) for x in s]
        m_new = []
        for hh in hs:
            m = functools.reduce(jnp.maximum, [m_blk[bi * heads + hh] for bi in range(len(blocks))])
            m_new.append(m if first else jnp.maximum(m_scr[hh], m))
        p = [jnp.exp((x - m_new[hh]).astype(BF16)) for x, (bi, hh) in zip(s, items)]
        pv = [jnp.dot(vts[bi][hh], x, preferred_element_type=F32) for x, (bi, hh) in zip(p, items)]
        for hh in hs:
            tot = functools.reduce(lambda x, y: x + y, [pv[bi * heads + hh] for bi in range(len(blocks))])
            if first:
                l_scr[hh] = tot[n:n + 1, :]
                acc_scr[hh] = tot[:n]
            else:
                alpha = jnp.exp(m_scr[hh] - m_new[hh])
                l_scr[hh] = l_scr[hh] * alpha + tot[n:n + 1, :]
                acc_scr[hh] = acc_scr[hh] * alpha + tot[:n]
            m_scr[hh] = m_new[hh]

    in_head = [(lane >= hh * n) & (lane < (hh + 1) * n) for hh in hs]
    qm = [(jnp.where(in_head[hh], q, 0.0) * scale).astype(BF16) for hh in hs]
    kmean = kmean_scr[...]
    kmean_heads = jnp.concatenate([jnp.where(in_head[hh], kmean, 0.0) for hh in hs], axis=0)
    gate_all = lax.dot_general(kmean_heads, q, _NT, precision=HI, preferred_element_type=F32)
    for hh in hs:
        gate = gate_all[hh * nblk:(hh + 1) * nblk, :]
        cnt = jnp.zeros((nblk, blk), jnp.int32)
        for mb in range(nblk - 1):
            gm = gate[mb:mb + 1, :]
            beats = (gm > gate) | ((gm == gate) & (mb < blk_id))
            cnt = cnt + jnp.where(beats, (mb < i).astype(jnp.int32), 0)
        sel = (blk_id < i) & (cnt < MOBA_TOP_K)
        sel_scr[hh] = jnp.where(sel, 1.0, 0.0)

    own = (i, [bias_ref[hh, 0] for hh in hs], None)

    def selected(b):
        return [sel_scr[hh, pl.ds(b, 1), :] for hh in hs]

    def far(b):
        return (b, [rb_ref[N_BUCKETS - 1, hp * heads + hh] for hh in hs], selected(b))

    n_far = jnp.maximum(i - 1, 0)
    attend([own, (n_far, [bias_ref[hh, 1] for hh in hs], selected(n_far))], True)

    def far_pair(j, carry):
        attend([far(2 * j), far(2 * j + 1)], False)
        return carry

    lax.fori_loop(0, lax.shift_right_logical(n_far, 1), far_pair, 0)

    @pl.when((n_far & 1) == 1)
    def _():
        attend([far(n_far - 1)], False)

    out_t = jnp.concatenate([acc_scr[hh] / l_scr[hh] for hh in range(heads)], axis=0)
    o_ref[...] = out_t.T


def _moba_prompt(q, k, v, bias_tiles, rel_bias, heads):
    nb, s, d = q.shape
    blk = MOBA_BLOCK
    n = HEAD_DIM
    gw = heads * n
    nblk = s // blk
    assert s % blk == 0 and gw % V7X_LANES == 0 and d % gw == 0
    qspec = pl.BlockSpec((None, blk, gw), lambda b, hp, i: (b, i, hp))
    kvspec = pl.BlockSpec((None, s, gw), lambda b, hp, i: (b, 0, hp))
    bspec = pl.BlockSpec((heads, 2, blk, blk), lambda b, hp, i: (hp, 0, 0, 0))
    return pl.pallas_call(
        _moba_prompt_body,
        grid=(nb, d // gw, nblk),
        in_specs=[pl.BlockSpec(memory_space=pltpu.SMEM), qspec, kvspec, kvspec, bspec],
        out_specs=qspec,
        out_shape=jax.ShapeDtypeStruct((nb, s, d), F32),
        scratch_shapes=[
            pltpu.VMEM((nblk, gw), F32),
            pltpu.VMEM((s, gw), BF16),
            pltpu.VMEM((nblk, heads, n + _ONES_ROWS, blk), BF16),
            pltpu.VMEM((heads, nblk, blk), F32),
            pltpu.VMEM((heads, 1, blk), F32),
            pltpu.VMEM((heads, 1, blk), F32),
            pltpu.VMEM((heads, n, blk), F32),
        ],
        compiler_params=_cparams(("arbitrary", "arbitrary", "arbitrary")),
        name="moba_prompt",
    )(rel_bias, q, k, v, bias_tiles)


_GATE_BLOCKS_PER_STEP = 8


def _moba_gate_body(pt_ref, qt_ref, *rest, ppb):
    top_ref, gate_scr = rest[-2:]
    page_refs = rest[:-2]
    step = pl.program_id(1)
    blocks_per_step = len(page_refs) // ppb
    nblocks = gate_scr.shape[0]
    rows = ppb * page_refs[0].shape[-1]
    for bi in range(blocks_per_step):
        ksum = page_refs[bi * ppb][...]
        for w in range(1, ppb):
            ksum = ksum + page_refs[bi * ppb + w][...]
        kmean = jnp.sum(ksum, axis=-1, keepdims=True) * (1.0 / rows)
        gate_scr[step * blocks_per_step + bi] = jnp.sum(qt_ref[...] * kmean, axis=1, keepdims=True)

    @pl.when(step == pl.num_programs(1) - 1)
    def _():
        g = gate_scr[...]
        idx = lax.broadcasted_iota(jnp.int32, g.shape, 0)
        for j in range(MOBA_TOP_K):
            mx = jnp.max(g, axis=0, keepdims=True)
            first = jnp.min(jnp.where(g == mx, idx, nblocks), axis=0, keepdims=True)
            top_ref[j] = first[0]
            g = jnp.where(idx == first, -jnp.inf, g)


def _moba_gate_topk(qt, cache_kt, layer, page_table, n_past_blocks):
    nb, nh, n, t = qt.shape
    page = cache_kt.shape[-1]
    ppb = MOBA_BLOCK // page
    pages_per_step = _tile(n_past_blocks, _GATE_BLOCKS_PER_STEP) * ppb

    def kspec(which):
        return pl.BlockSpec((None, None, nh, n, page),
                            lambda b, j, pt: (layer, pt[b, j * pages_per_step + which], 0, 0, 0))

    grid_spec = pltpu.PrefetchScalarGridSpec(
        num_scalar_prefetch=1,
        grid=(nb, n_past_blocks * ppb // pages_per_step),
        in_specs=[pl.BlockSpec((None, nh, n, t), lambda b, j, pt: (b, 0, 0, 0))]
        + [kspec(w) for w in range(pages_per_step)],
        out_specs=pl.BlockSpec((None, MOBA_TOP_K, nh, 1, t), lambda b, j, pt: (b, 0, 0, 0, 0)),
        scratch_shapes=[pltpu.VMEM((n_past_blocks, nh, 1, t), F32)],
    )
    return pl.pallas_call(
        functools.partial(_moba_gate_body, ppb=ppb),
        grid_spec=grid_spec,
        out_shape=jax.ShapeDtypeStruct((nb, MOBA_TOP_K, nh, 1, t), jnp.int32),
        compiler_params=_cparams(("arbitrary", "arbitrary")),
        name="moba_gate",
    )(page_table, qt, *([cache_kt] * pages_per_step))


def _moba_sample_body(top_ref, pt_ref, rb_ref, q_ref, kn_ref, vn_ref, ck_ref, cv_ref, o_ref,
                      kbuf, vbuf, sems, *, layer, n_pages, past_len, nh):
    b = pl.program_id(0)
    h = pl.program_id(1)
    t, n = q_ref.shape
    page = kbuf.shape[-1]
    ppb = MOBA_BLOCK // page
    n_sel = MOBA_TOP_K * ppb
    scale = HEAD_DIM ** -0.5

    step = b * nh + h
    cur = lax.rem(step, 2)

    def block_of(ti, j, bb=b, hh=h):
        return top_ref[((bb * MOBA_TOP_K + j) * nh + hh) * t + ti]

    def copies(bb, hh, half, ti, j, p):
        phys = pt_ref[bb * n_pages + block_of(ti, j, bb, hh) * ppb + p]
        slot = (ti * MOBA_TOP_K + j) * ppb + p
        return (pltpu.make_async_copy(ck_ref.at[layer, phys, hh], kbuf.at[half, slot], sems.at[half, 0]),
                pltpu.make_async_copy(cv_ref.at[layer, phys, hh], vbuf.at[half, slot], sems.at[half, 1]))

    every = [(ti, j, p) for ti in range(t) for j in range(MOBA_TOP_K) for p in range(ppb)]

    def start_all(bb, hh, half):
        for idx in every:
            for cp in copies(bb, hh, half, *idx):
                cp.start()

    @pl.when(step == 0)
    def _():
        start_all(b, h, cur)

    @pl.when(step + 1 < pl.num_programs(0) * nh)
    def _():
        wraps = h == nh - 1
        start_all(jnp.where(wraps, b + 1, b), jnp.where(wraps, 0, h + 1), 1 - cur)

    for idx in every:
        for cp in copies(b, h, cur, *idx):
            cp.wait()

    qs = (q_ref[...] * scale).astype(BF16)
    n_slots = t * n_sel
    n_keys = n_slots * page
    seg = n_sel * page
    kt_all = jnp.concatenate([kbuf[cur, sl] for sl in range(n_slots)], axis=1)
    vt_all = jnp.concatenate([vbuf[cur, sl] for sl in range(n_slots)], axis=1)
    s_all = _mm(qs, kt_all)
    row = lax.broadcasted_iota(jnp.int32, (t, seg), 0)
    offs = lax.broadcasted_iota(jnp.int32, (1, MOBA_BLOCK), 1)
    s = jnp.zeros((t, seg), F32)
    kpos = jnp.zeros((t, seg), jnp.int32)
    for ti in range(t):
        s = jnp.where(row == ti, s_all[:, ti * seg:(ti + 1) * seg], s)
        pos = jnp.concatenate([block_of(ti, j) * MOBA_BLOCK + offs for j in range(MOBA_TOP_K)], axis=1)
        kpos = jnp.where(row == ti, pos, kpos)
    rel = (past_len + row) - kpos
    s = s + _bias_of_distance(jnp.maximum(rel, 0), rb_ref, h)
    r_o = lax.broadcasted_iota(jnp.int32, (t, t), 0)
    c_o = lax.broadcasted_iota(jnp.int32, (t, t), 1)
    rel_o = r_o - c_o
    s_own = _mm(qs, kn_ref[...], _NT) + _bias_of_distance(jnp.maximum(rel_o, 0), rb_ref, h)
    s_own = jnp.where(rel_o >= 0, s_own, NEG_BIG)
    m = jnp.maximum(jnp.max(s, axis=-1, keepdims=True), jnp.max(s_own, axis=-1, keepdims=True))
    p = jnp.exp(s - m)
    p_own = jnp.exp(s_own - m)
    denom = jnp.sum(p, axis=-1, keepdims=True) + jnp.sum(p_own, axis=-1, keepdims=True)
    p_all = jnp.concatenate([jnp.where(row == ti, p, 0.0) for ti in range(t)], axis=1)
    o_ref[...] = (_mm(p_all, vt_all, _NT) + _mm(p_own, vn_ref[...])) / denom


def _moba_sample(q, k_new, v_new, top, page_table, rel_bias, cache_kt, cache_vt, layer, past_len):
    nb, nh, t, n = q.shape
    page = cache_kt.shape[-1]
    n_pages = page_table.shape[1]
    ppb = MOBA_BLOCK // page
    tok = pl.BlockSpec((None, None, t, n), lambda b, h, *_: (b, h, 0, 0))
    any_spec = pl.BlockSpec(memory_space=pl.ANY)
    grid_spec = pltpu.PrefetchScalarGridSpec(
        num_scalar_prefetch=2,
        grid=(nb, nh),
        in_specs=[pl.BlockSpec(memory_space=pltpu.SMEM), tok, tok, tok, any_spec, any_spec],
        out_specs=tok,
        scratch_shapes=[
            pltpu.VMEM((2, t * MOBA_TOP_K * ppb, n, page), F32),
            pltpu.VMEM((2, t * MOBA_TOP_K * ppb, n, page), F32),
            pltpu.SemaphoreType.DMA((2, 2)),
        ],
    )
    return pl.pallas_call(
        functools.partial(_moba_sample_body, layer=layer, n_pages=n_pages, past_len=past_len, nh=nh),
        grid_spec=grid_spec,
        out_shape=jax.ShapeDtypeStruct((nb, nh, t, n), F32),
        compiler_params=_cparams(("arbitrary", "arbitrary")),
        name="moba_sample",
    )(top.reshape(-1), page_table.reshape(-1), rel_bias, q, k_new, v_new, cache_kt, cache_vt)


def _tile(m, pref):
    return pref if m % pref == 0 else m


def kernel(x_prompt, x_sample, state_wkv, state_shift, cache_k, cache_v, page_table, ln1, ln2, ln_f, mlp_up, mlp_down, rwkv_mix, rwkv_w_r, rwkv_w_k, rwkv_w_v, rwkv_w_o, rwkv_w0, rwkv_w1, rwkv_w2, rwkv_a0, rwkv_a1, rwkv_a2, rwkv_g1, rwkv_g2, rwkv_k_k, rwkv_k_a, rwkv_r_k, rwkv_ln_w, rwkv_ln_b, moba_w_qkv, moba_w_o, rel_bias):
    nb, s, d = x_prompt.shape
    nbd, t, _ = x_sample.shape
    n = HEAD_DIM
    nh = d // n
    depth = ln1.shape[0]
    assert depth == 2 and rwkv_mix.shape[0] == 1 and moba_w_qkv.shape[0] == 1
    page = cache_k.shape[2]
    past_len = page_table.shape[1] * page
    assert past_len % MOBA_BLOCK == 0 and MOBA_BLOCK % page == 0
    n_past_blocks = past_len // MOBA_BLOCK
    assert n_past_blocks >= MOBA_TOP_K and s % MOBA_BLOCK == 0

    row = lambda a: a.reshape(1, -1)
    bf = lambda a: a.astype(BF16)
    lnf = row(ln_f)

    proj_w = (row(ln1[0]), rwkv_mix[0], bf(rwkv_w_r[0]), bf(rwkv_w_k[0]), bf(rwkv_w_v[0]),
              bf(rwkv_w1[0]), bf(rwkv_a1[0]), bf(rwkv_g1[0]), bf(rwkv_w2[0]), bf(rwkv_a2[0]),
              bf(rwkv_g2[0]), row(rwkv_w0[0]), row(rwkv_a0[0]))
    scan_w = (row(rwkv_k_k[0]), row(rwkv_k_a[0]), row(rwkv_r_k[0]), row(rwkv_ln_w[0]), row(rwkv_ln_b[0]))
    mlp0 = (bf(rwkv_w_o[0]), row(ln2[0]), bf(mlp_up[0]), bf(mlp_down[0]), lnf)
    mlp1 = (bf(moba_w_o[0]), row(ln2[1]), bf(mlp_up[1]), bf(mlp_down[1]), lnf)

    def rwkv_layer(x, hprev, s0, tm, chunk, tm_mlp):
        streams, shift = _rwkv_proj(x, hprev, *proj_w, tm=tm)
        z, s_out = _rwkv_scan(*streams, s0, *scan_w, chunk=chunk,
                              chunks_per_step=_tile(x.shape[1] // chunk, 4), heads_per_step=_tile(nh, 16))
        y = _out_mlp(x.reshape(-1, d), z.reshape(-1, d), *mlp0, tm=tm_mlp, final_norm=False)
        return y, shift, s_out

    yp, shift_p, wkv_p = rwkv_layer(x_prompt, jnp.zeros((nb, d), F32), jnp.zeros((nb, nh, n, n), F32),
                                    _tile(s, 256), _tile(s, 64), _tile(nb * s, 512))
    ys, shift_s, wkv_s = rwkv_layer(x_sample, state_shift[0], state_wkv[0], t, t, nbd * t)

    w_qkv = bf(moba_w_qkv[0])
    qp, kp, vp, kp_t, vp_t = _qkv(yp.reshape(nb, s, d), row(ln1[1]), w_qkv, _tile(s, 512), True)
    qs, ks, vs = _qkv(ys.reshape(1, nbd * t, d), row(ln1[1]), w_qkv, nbd * t, False)
    bias_tiles = _moba_bias_tiles(rel_bias, MOBA_BLOCK)
    attn_p = _moba_prompt(qp, kp, vp, bias_tiles, rel_bias, heads=_tile(nh, 4))
    per_head = lambda a: a.reshape(nbd, t, nh, n)
    cache_kt = cache_k.transpose(0, 1, 3, 4, 2)
    cache_vt = cache_v.transpose(0, 1, 3, 4, 2)
    top = _moba_gate_topk(per_head(qs).transpose(0, 2, 3, 1), cache_kt, 0, page_table, n_past_blocks)
    heads_first = lambda a: per_head(a).transpose(0, 2, 1, 3)
    attn_s = _moba_sample(heads_first(qs), heads_first(ks), heads_first(vs), top, page_table,
                          rel_bias, cache_kt, cache_vt, 0, past_len)
    attn_s = attn_s.transpose(0, 2, 1, 3).reshape(nbd * t, d)
    y_prompt = _out_mlp(yp, attn_p.reshape(-1, d), *mlp1, tm=_tile(nb * s, 512), final_norm=True)
    y_sample = _out_mlp(ys, attn_s, *mlp1, tm=nbd * t, final_norm=True)

    seq_last = lambda a: a.reshape(1, nb, nh, n, s).transpose(0, 1, 4, 2, 3)
    return (y_prompt.reshape(nb, s, d), y_sample.reshape(nbd, t, d),
            wkv_p[None], shift_p[None], seq_last(kp_t), seq_last(vp_t),
            wkv_s[None], shift_s[None],
            ks.reshape(1, nbd, t, nh, n), vs.reshape(1, nbd, t, nh, n))
```

```python
import functools
import math

import numpy as np
import jax
import jax.numpy as jnp
from jax import lax
from jax.experimental import pallas as pl
from jax.experimental.pallas import tpu as pltpu

F32 = jnp.float32
BF16 = jnp.bfloat16

HEAD_DIM = 64
GN_EPS = 64e-5
RMS_EPS = 1e-6
NEG_BIG = -1e30
MOBA_BLOCK = 256
MOBA_TOP_K = 3
N_BUCKETS = 32
MAX_DISTANCE = 128
KK_EPS = 1e-24

V7X_LANES = 128
V7X_SUBLANES = 8
V7X_VMEM_LIMIT_BYTES = 52 * 1024 * 1024

HI = lax.Precision.HIGHEST
_NN = (((1,), (0,)), ((), ()))
_NT = (((1,), (1,)), ((), ()))
_TN = (((0,), (0,)), ((), ()))


def _cparams(sem):
    return pltpu.CompilerParams(dimension_semantics=sem, vmem_limit_bytes=V7X_VMEM_LIMIT_BYTES)


def _const_spec(shape):
    nd = len(shape)
    return pl.BlockSpec(shape, lambda *_: (0,) * nd)


def _rms(x, g):
    return x * lax.rsqrt(jnp.mean(x * x, axis=-1, keepdims=True) + RMS_EPS) * g


def _mm(a, b, dims=_NN):
    return lax.dot_general(a.astype(BF16), b.astype(BF16), dims, preferred_element_type=F32)


def _rwkv_proj_body(x_ref, hprev_ref, ln_ref, mix_ref, wr_ref, wk_ref, wv_ref,
                    w1_ref, a1_ref, g1_ref, w2_ref, a2_ref, g2_ref, w0_ref, a0_ref,
                    r_ref, k_ref, v_ref, wl_ref, a_ref, g_ref, shift_ref, hbuf):
    j = pl.program_id(1)
    tm = x_ref.shape[0]
    pad = V7X_SUBLANES
    h = _rms(x_ref[...], ln_ref[...])

    @pl.when(j == 0)
    def _():
        hbuf[pad - 1:pad, :] = hprev_ref[...]

    hbuf[pad:pad + tm, :] = h
    dx = hbuf[pad - 1:pad - 1 + tm, :] - h
    hbuf[pad - 1:pad, :] = h[tm - 1:tm, :]
    shift_ref[...] = h[tm - 1:tm, :]

    def mixed(i):
        return (h + dx * mix_ref[i:i + 1, :]).astype(BF16)

    r_ref[...] = jnp.dot(mixed(0), wr_ref[...], preferred_element_type=F32)
    k_ref[...] = jnp.dot(mixed(2), wk_ref[...], preferred_element_type=F32)
    v_ref[...] = jnp.dot(mixed(3), wv_ref[...], preferred_element_type=F32)
    tw = jnp.tanh(jnp.dot(mixed(1), w1_ref[...], preferred_element_type=F32))
    wpre = w0_ref[...] + _mm(tw, w2_ref[...])
    wl_ref[...] = -math.exp(-0.5) * jax.nn.sigmoid(wpre)
    ta = jnp.dot(mixed(4), a1_ref[...], preferred_element_type=F32)
    a_ref[...] = jax.nn.sigmoid(a0_ref[...] + _mm(ta, a2_ref[...]))
    tg = jax.nn.sigmoid(jnp.dot(mixed(5), g1_ref[...], preferred_element_type=F32))
    g_ref[...] = _mm(tg, g2_ref[...])


def _rwkv_proj(x, hprev, ln, mix, wr, wk, wv, w1, a1, g1, w2, a2, g2, w0, a0, tm):
    nb, t, d = x.shape
    assert t % tm == 0
    tok = pl.BlockSpec((None, tm, d), lambda b, j: (b, j, 0))
    row = pl.BlockSpec((None, 1, d), lambda b, j: (b, 0, 0))
    ins = [x, hprev.reshape(nb, 1, d), ln, mix, wr, wk, wv, w1, a1, g1, w2, a2, g2, w0, a0]
    in_specs = [tok, row] + [_const_spec(a.shape) for a in ins[2:]]
    big = jax.ShapeDtypeStruct((nb, t, d), F32)
    outs = pl.pallas_call(
        _rwkv_proj_body,
        grid=(nb, t // tm),
        in_specs=in_specs,
        out_specs=[tok] * 6 + [row],
        out_shape=[big] * 6 + [jax.ShapeDtypeStruct((nb, 1, d), F32)],
        scratch_shapes=[pltpu.VMEM((tm + 2 * V7X_SUBLANES, d), F32)],
        compiler_params=_cparams(("arbitrary", "arbitrary")),
        name="rwkv_proj",
    )(*ins)
    return outs[:6], outs[6].reshape(nb, d)


_INV_BASE = 8

def _rwkv_chunks(r, k, v, wl, a, g, sbd, kkp, kap, rkp, lnw, lnb, m):
    c, lanes = r[0].shape
    pairs = len(sbd)
    n_chunks = len(r) // pairs
    kkp, kap, rkp, lnw, lnb = (x * n_chunks for x in (kkp, kap, rkp, lnw, lnb))
    ch0, th0, strict, incl, eye_cat, tril_bf, bd128, n_square, rowb, colb, levels = m
    zero = jnp.zeros((), F32)
    each = lambda fn, *ls: [fn(*xs) for xs in zip(*ls)]

    def seg_sum(x):
        s0 = jnp.sum(jnp.where(ch0, x, zero), axis=-1, keepdims=True)
        s1 = jnp.sum(jnp.where(ch0, zero, x), axis=-1, keepdims=True)
        return jnp.where(ch0, s0, s1)

    def bd_chan(x):
        return jnp.concatenate([jnp.where(ch0, x, zero), jnp.where(ch0, zero, x)], axis=0)

    def bd_chan_swapped(x):
        return jnp.concatenate([jnp.where(ch0, zero, x), jnp.where(ch0, x, zero)], axis=0)

    def bd_time(x):
        return jnp.concatenate([jnp.where(th0, x, zero), jnp.where(th0, zero, x)], axis=0)

    def normalised(k_, kkp_):
        kk_ = k_ * kkp_
        return kk_ * lax.rsqrt(jnp.maximum(seg_sum(kk_ * kk_), KK_EPS))

    def cumulative(wl_):
        w_hi = wl_.astype(BF16)
        w_r1 = wl_ - w_hi.astype(F32)
        w_mid = w_r1.astype(BF16)
        w_lo = (w_r1 - w_mid.astype(F32)).astype(BF16)
        return (jnp.dot(tril_bf, w_hi, preferred_element_type=F32)
                + (jnp.dot(tril_bf, w_mid, preferred_element_type=F32)
                   + jnp.dot(tril_bf, w_lo, preferred_element_type=F32)))

    def independent(r, k, v, wl, a, kkp, kap):
        kk = each(normalised, k, kkp)
        k2 = each(lambda k_, a_, kap_: k_ * (1.0 + (a_ - 1.0) * kap_), k, a, kap)
        b = each(lambda kk_, a_: kk_ * a_, kk, a)
        cum = []
        for j in range(n_chunks):
            full = cumulative(jnp.concatenate(wl[j * pairs:(j + 1) * pairs], axis=1))
            cum += [full[:, pi * lanes:(pi + 1) * lanes] for pi in range(pairs)]
        cum_end = [x[c - 1:c, :] for x in cum]
        lhs = each(lambda kk_, r_, cum_, wl_: jnp.concatenate(
            [kk_ * jnp.exp(cum_ - wl_), r_ * jnp.exp(cum_)], axis=0), kk, r, cum, wl)
        inv_w = [jnp.exp(-x) for x in cum]
        bt = each(lambda b_, w_: b_ * w_, b, inv_w)
        kt = each(lambda k_, w_: k_ * w_, k2, inv_w)
        to_end = each(lambda e_, cum_: jnp.exp(e_ - cum_), cum_end, cum)
        rhs_end = each(lambda b_, k_, w_: jnp.concatenate([b_ * w_, k_ * w_], axis=0), b, k2, to_end)

        if (2 * c) % V7X_LANES == 0:
            g01 = each(lambda l_, bt_, kt_: _mm(l_, jnp.concatenate(
                [jnp.where(ch0, bt_, zero), jnp.where(ch0, kt_, zero),
                 jnp.where(ch0, zero, kt_), jnp.where(ch0, zero, bt_)], axis=0), _NT), lhs, bt, kt)
            g0 = [x[:, :2 * c] for x in g01]
            g1 = [x[:, 2 * c:] for x in g01]
        else:
            g0 = each(lambda l_, bt_, kt_: _mm(jnp.where(ch0, l_, zero),
                                               jnp.concatenate([bt_, kt_], axis=0), _NT), lhs, bt, kt)
            g1 = each(lambda l_, bt_, kt_: _mm(jnp.where(ch0, zero, l_),
                                               jnp.concatenate([kt_, bt_], axis=0), _NT), lhs, bt, kt)
        mab_c = each(lambda x, y: jnp.where(strict, jnp.where(th0, x[:c], y[:c]), zero), g0, g1)
        mak_s = each(lambda x, y: jnp.where(strict, jnp.where(th0, y[:c], x[:c]), zero), g0, g1)
        nab_c = each(lambda x, y: jnp.where(incl, jnp.where(th0, x[c:], y[c:]), zero), g0, g1)
        nak_s = each(lambda x, y: jnp.where(incl, jnp.where(th0, y[c:], x[c:]), zero), g0, g1)

        mn = each(lambda m_, n_, v_: _mm(jnp.concatenate([m_, n_], axis=0), bd_chan_swapped(v_)),
                  mak_s, nak_s, v)

        xp = [jnp.where(rowb == colb, -x, zero) for x in mab_c]
        p = [eye_cat + x for x in xp]
        for _ in range(n_square):
            xp = each(lambda x: _mm(x, bd_time(x)), xp)
            p = each(lambda p_, x: p_ + _mm(p_, bd_time(x)), p, xp)
        for lev in levels:
            half = each(lambda p_, m_: _mm(p_, bd_time(jnp.where(lev, m_, zero))), p, mab_c)
            p = each(lambda p_, h_: p_ - _mm(h_, bd_time(p_)), p, half)
        return k2, cum_end, lhs, rhs_end, nab_c, mn, p

    k2, cum_end, lhs, rhs_end, nab_c, mn, p = independent(r, k, v, wl, a, kkp, kap)

    y = []
    for j in range(n_chunks):
        it = slice(j * pairs, (j + 1) * pairs)
        ls = each(lambda l_, s_: _mm(l_, s_, _NT), lhs[it], sbd)
        sa = each(lambda p_, ls_, mn_: -_mm(p_, bd_chan(ls_[:c] + mn_[:c])), p[it], ls, mn[it])
        upd = each(lambda sa_, v_, e_: _mm(jnp.concatenate([sa_, v_], axis=0), e_, _TN),
                   sa, v[it], rhs_end[it])
        sbd = each(lambda s_, e_, u_: s_ * jnp.exp(e_) + jnp.where(bd128, u_, zero),
                   sbd, cum_end[it], upd)
        y += each(lambda ls_, n_, sa_, mn_: ls_[c:] + _mm(n_, bd_chan(sa_)) + mn_[c:],
                  ls, nab_c[it], sa, mn[it])
    sbd_new = sbd

    inv_n = 1.0 / HEAD_DIM

    def finish(y_, r_, k2_, v_, g_, rkp_, lnw_, lnb_):
        yc = y_ - seg_sum(y_) * inv_n
        var = seg_sum(yc * yc) * inv_n
        yn = yc * lax.rsqrt(var + GN_EPS) * lnw_ + lnb_
        return (yn + seg_sum(r_ * k2_ * rkp_) * v_) * g_

    return each(finish, y, r, k2, v, g, rkp, lnw, lnb), sbd_new


def _rwkv_scan_body(r_ref, k_ref, v_ref, wl_ref, a_ref, g_ref, s0_ref,
                    kkp_ref, kap_ref, rkp_ref, lnw_ref, lnb_ref,
                    z_ref, sout_ref, s_scr, *, chunk):
    ci = pl.program_id(2)
    c = chunk
    n_chunks = r_ref.shape[0] // c
    n = HEAD_DIM
    w = 2 * n
    pairs = r_ref.shape[1] // w
    zpad = jnp.zeros((n, n), F32)

    @pl.when(ci == 0)
    def _():
        for pi in range(pairs):
            top = jnp.concatenate([s0_ref[2 * pi], zpad], axis=1)
            bot = jnp.concatenate([zpad, s0_ref[2 * pi + 1]], axis=1)
            s_scr[pi] = jnp.concatenate([top, bot], axis=0)

    lane = lax.broadcasted_iota(jnp.int32, (1, w), 1)
    rowt = lax.broadcasted_iota(jnp.int32, (c, 2 * c), 0)
    colt = lax.broadcasted_iota(jnp.int32, (c, 2 * c), 1)
    th0 = colt < c
    tcol = jnp.where(th0, colt, colt - c)
    row_c = lax.broadcasted_iota(jnp.int32, (c, c), 0)
    col_c = lax.broadcasted_iota(jnp.int32, (c, c), 1)
    r128 = lax.broadcasted_iota(jnp.int32, (w, w), 0)
    c128 = lax.broadcasted_iota(jnp.int32, (w, w), 1)
    base = min(c, _INV_BASE)
    base_log = int(math.log2(base))
    assert 1 << base_log == base and c % base == 0 and (c // base) & (c // base - 1) == 0
    levels = []
    size_log = base_log
    while (1 << size_log) < c:
        rb = lax.shift_right_logical(rowt, size_log)
        cb = lax.shift_right_logical(tcol, size_log)
        levels.append(((rb & 1) == 1) & (cb == rb - 1))
        size_log += 1
    masks = (lane < n, th0, tcol < rowt, tcol <= rowt, jnp.where(tcol == rowt, 1.0, 0.0).astype(F32),
             jnp.where(col_c <= row_c, 1.0, 0.0).astype(BF16), (r128 < n) == (c128 < n),
             base_log - 1, lax.shift_right_logical(rowt, base_log),
             lax.shift_right_logical(tcol, base_log), levels)
    lanes = [slice(pi * w, (pi + 1) * w) for pi in range(pairs)]
    rows = [slice(j * c, (j + 1) * c) for j in range(n_chunks)]
    per_item = lambda ref: [ref[rw, sl] for rw in rows for sl in lanes]
    per_pair = lambda ref: [ref[:, sl] for sl in lanes]
    z, s_new = _rwkv_chunks(
        per_item(r_ref), per_item(k_ref), per_item(v_ref), per_item(wl_ref), per_item(a_ref),
        per_item(g_ref), [s_scr[pi] for pi in range(pairs)], per_pair(kkp_ref), per_pair(kap_ref),
        per_pair(rkp_ref), per_pair(lnw_ref), per_pair(lnb_ref), masks)
    for j in range(n_chunks):
        for pi in range(pairs):
            z_ref[rows[j], lanes[pi]] = z[j * pairs + pi]
    for pi in range(pairs):
        s_scr[pi] = s_new[pi]

    @pl.when(ci == pl.num_programs(2) - 1)
    def _():
        for pi in range(pairs):
            s = s_scr[pi]
            sout_ref[2 * pi] = s[:n, :n]
            sout_ref[2 * pi + 1] = s[n:, n:]


def _rwkv_scan(r, k, v, wl, a, g, s0, kkp, kap, rkp, lnw, lnb, chunk, chunks_per_step, heads_per_step):
    nb, t, d = r.shape
    n = HEAD_DIM
    nh = d // n
    gw = heads_per_step * n
    rows = chunk * chunks_per_step
    assert t % rows == 0 and nh % heads_per_step == 0 and gw % V7X_LANES == 0
    tok = pl.BlockSpec((None, rows, gw), lambda b, hg, ci: (b, ci, hg))
    st = pl.BlockSpec((None, heads_per_step, n, n), lambda b, hg, ci: (b, hg, 0, 0))
    par = pl.BlockSpec((1, gw), lambda b, hg, ci: (0, hg))
    z, s_out = pl.pallas_call(
        functools.partial(_rwkv_scan_body, chunk=chunk),
        grid=(nb, nh // heads_per_step, t // rows),
        in_specs=[tok] * 6 + [st] + [par] * 5,
        out_specs=[tok, st],
        out_shape=[jax.ShapeDtypeStruct((nb, t, d), F32),
                   jax.ShapeDtypeStruct((nb, nh, n, n), F32)],
        scratch_shapes=[pltpu.VMEM((heads_per_step // 2, 2 * n, 2 * n), F32)],
        compiler_params=_cparams(("arbitrary", "arbitrary", "arbitrary")),
        name="rwkv_scan",
    )(r, k, v, wl, a, g, s0, kkp, kap, rkp, lnw, lnb)
    return z, s_out


def _out_mlp_body(x_ref, z_ref, wo_ref, ln2_ref, up_ref, dn_ref, lnf_ref, y_ref, *, ff_chunk, final_norm):
    yp = x_ref[...] + jnp.dot(z_ref[...].astype(BF16), wo_ref[...], preferred_element_type=F32)
    h = _rms(yp, ln2_ref[...]).astype(BF16)
    acc = yp
    for c0 in range(0, up_ref.shape[1], ff_chunk):
        u = jnp.dot(h, up_ref[:, c0:c0 + ff_chunk], preferred_element_type=F32)
        u = jnp.square(jnp.maximum(u, 0.0)).astype(BF16)
        acc = acc + jnp.dot(u, dn_ref[c0:c0 + ff_chunk, :], preferred_element_type=F32)
    y_ref[...] = _rms(acc, lnf_ref[...]) if final_norm else acc


def _out_mlp(x, z, wo, ln2, up, dn, lnf, tm, final_norm):
    m, d = x.shape
    assert m % tm == 0
    tok = pl.BlockSpec((tm, d), lambda i: (i, 0))
    ins = [x, z, wo, ln2, up, dn, lnf]
    return pl.pallas_call(
        functools.partial(_out_mlp_body, ff_chunk=1024, final_norm=final_norm),
        grid=(m // tm,),
        in_specs=[tok, tok] + [_const_spec(a.shape) for a in ins[2:]],
        out_specs=tok,
        out_shape=jax.ShapeDtypeStruct((m, d), F32),
        compiler_params=_cparams(("arbitrary",)),
        name="out_mlp",
    )(*ins)


def _qkv_body(x_ref, ln_ref, w_ref, q_ref, k_ref, v_ref, *t_refs):
    d = x_ref.shape[1]
    h = _rms(x_ref[...], ln_ref[...]).astype(BF16)
    q_ref[...] = jnp.dot(h, w_ref[:, 0:d], preferred_element_type=F32)
    k = jnp.dot(h, w_ref[:, d:2 * d], preferred_element_type=F32)
    v = jnp.dot(h, w_ref[:, 2 * d:3 * d], preferred_element_type=F32)
    k_ref[...] = k
    v_ref[...] = v
    if t_refs:
        kt_ref, vt_ref = t_refs
        kt_ref[...] = k.T
        vt_ref[...] = v.T


def _qkv(x, ln, w, tm, transposed):
    nb, s, d = x.shape
    assert s % tm == 0
    tok = pl.BlockSpec((None, tm, d), lambda b, j: (b, j, 0))
    tok_t = pl.BlockSpec((None, d, tm), lambda b, j: (b, 0, j))
    sds = jax.ShapeDtypeStruct((nb, s, d), F32)
    sds_t = jax.ShapeDtypeStruct((nb, d, s), F32)
    return pl.pallas_call(
        _qkv_body,
        grid=(nb, s // tm),
        in_specs=[tok, _const_spec(ln.shape), _const_spec(w.shape)],
        out_specs=[tok] * 3 + ([tok_t] * 2 if transposed else []),
        out_shape=[sds] * 3 + ([sds_t] * 2 if transposed else []),
        compiler_params=_cparams(("arbitrary", "arbitrary")),
        name="moba_qkv",
    )(x, ln, w)


def _bucket_starts():
    n = np.arange(MAX_DISTANCE, dtype=np.float32)
    max_exact = N_BUCKETS // 2
    nf = np.maximum(n, 1.0).astype(np.float32)
    large = max_exact + (np.log(nf / np.float32(max_exact)) / np.float32(math.log(MAX_DISTANCE / max_exact))
                         * np.float32(N_BUCKETS - max_exact)).astype(np.int32)
    bucket = np.where(n < max_exact, n.astype(np.int32), np.minimum(large, N_BUCKETS - 1))
    starts = [int(np.argmax(bucket >= b)) if (bucket >= b).any() else MAX_DISTANCE
              for b in range(N_BUCKETS)]
    starts[N_BUCKETS - 1] = min(starts[N_BUCKETS - 1], MAX_DISTANCE)
    return starts


_BUCKET_STARTS = _bucket_starts()


def _bias_of_distance(rel, rb_ref, head):
    bias = jnp.full(rel.shape, rb_ref[0, head], F32)
    for b in range(1, N_BUCKETS):
        bias = jnp.where(rel >= _BUCKET_STARTS[b], rb_ref[b, head], bias)
    return bias


def _moba_bias_body(rb_ref, o_ref):
    h = pl.program_id(0)
    blk = o_ref.shape[-1]
    kj = lax.broadcasted_iota(jnp.int32, (blk, blk), 0)
    qi = lax.broadcasted_iota(jnp.int32, (blk, blk), 1)
    rel = qi - kj
    own = _bias_of_distance(jnp.maximum(rel, 0), rb_ref, h)
    o_ref[0] = jnp.where(rel >= 0, own, NEG_BIG)
    o_ref[1] = _bias_of_distance(rel + blk, rb_ref, h)


def _moba_bias_tiles(rel_bias, blk):
    nh = rel_bias.shape[1]
    return pl.pallas_call(
        _moba_bias_body,
        grid=(nh,),
        in_specs=[pl.BlockSpec(memory_space=pltpu.SMEM)],
        out_specs=pl.BlockSpec((None, 2, blk, blk), lambda h: (h, 0, 0, 0)),
        out_shape=jax.ShapeDtypeStruct((nh, 2, blk, blk), F32),
        compiler_params=_cparams(("arbitrary",)),
        name="moba_bias",
    )(rel_bias)


_ONES_ROWS = 16


def _moba_prompt_body(rb_ref, q_ref, k_ref, v_ref, bias_ref, o_ref,
                      kmean_scr, kb_scr, vt_scr, sel_scr, m_scr, l_scr, acc_scr):
    i = pl.program_id(2)
    hp = pl.program_id(1)
    blk = q_ref.shape[0]
    n = HEAD_DIM
    nblk = k_ref.shape[0] // blk
    heads = q_ref.shape[1] // n
    scale = HEAD_DIM ** -0.5

    @pl.when(i == 0)
    def _():
        ones = jnp.ones((_ONES_ROWS, blk), BF16)
        for b in range(nblk):
            kb = k_ref[b * blk:(b + 1) * blk, :]
            kmean_scr[b:b + 1, :] = jnp.mean(kb, axis=0, keepdims=True)
            kb_scr[b * blk:(b + 1) * blk, :] = kb.astype(BF16)
            vt = v_ref[b * blk:(b + 1) * blk, :].T.astype(BF16)
            for hh in range(heads):
                vt_scr[b, hh] = jnp.concatenate([vt[hh * n:(hh + 1) * n, :], ones], axis=0)

    lane = lax.broadcasted_iota(jnp.int32, (1, heads * n), 1)
    q = q_ref[...]
    blk_id = lax.broadcasted_iota(jnp.int32, (nblk, blk), 0)

    hs = range(heads)

    def attend(blocks, first):
        ks, vts, biases, sels = [], [], [], []
        for b, bias, sel_row in blocks:
            if sel_row is not None and bias[0].ndim == 0:
                bias = [jnp.where(sel_row[hh] > 0, bias[hh], NEG_BIG) for hh in hs]
                sel_row = None
            ks.append(kb_scr[pl.ds(pl.multiple_of(b * blk, blk), blk), :])
            vts.append(vt_scr[b])
            biases.append(bias)
            sels.append(sel_row)
        items = [(bi, hh) for bi in range(len(blocks)) for hh in hs]
        s = [lax.dot_general(ks[bi], qm[hh], _NT, preferred_element_type=F32) + biases[bi][hh]
             for bi, hh in items]
        s = [x if sels[bi] is None else jnp.where(sels[bi][hh] > 0, x, NEG_BIG)
             for x, (bi, hh) in zip(s, items)]
        m_blk = [jnp.max(x, axis=0, keepdims=True) for x in s]
        m_new = []
        for hh in hs:
            m = functools.reduce(jnp.maximum, [m_blk[bi * heads + hh] for bi in range(len(blocks))])
            m_new.append(m if first else jnp.maximum(m_scr[hh], m))
        p = [jnp.exp((x - m_new[hh]).astype(BF16)) for x, (bi, hh) in zip(s, items)]
        pv = [jnp.dot(vts[bi][hh], x, preferred_element_type=F32) for x, (bi, hh) in zip(p, items)]
        for hh in hs:
            tot = functools.reduce(lambda x, y: x + y, [pv[bi * heads + hh] for bi in range(len(blocks))])
            if first:
                l_scr[hh] = tot[n:n + 1, :]
                acc_scr[hh] = tot[:n]
            else:
                alpha = jnp.exp(m_scr[hh] - m_new[hh])
                l_scr[hh] = l_scr[hh] * alpha + tot[n:n + 1, :]
                acc_scr[hh] = acc_scr[hh] * alpha + tot[:n]
            m_scr[hh] = m_new[hh]

    in_head = [(lane >= hh * n) & (lane < (hh + 1) * n) for hh in hs]
    qm = [(jnp.where(in_head[hh], q, 0.0) * scale).astype(BF16) for hh in hs]
    kmean = kmean_scr[...]
    kmean_heads = jnp.concatenate([jnp.where(in_head[hh], kmean, 0.0) for hh in hs], axis=0)
    gate_all = lax.dot_general(kmean_heads, q, _NT, precision=HI, preferred_element_type=F32)
    for hh in hs:
        gate = gate_all[hh * nblk:(hh + 1) * nblk, :]
        cnt = jnp.zeros((nblk, blk), jnp.int32)
        for mb in range(nblk - 1):
            gm = gate[mb:mb + 1, :]
            beats = (gm > gate) | ((gm == gate) & (mb < blk_id))
            cnt = cnt + jnp.where(beats, (mb < i).astype(jnp.int32), 0)
        sel = (blk_id < i) & (cnt < MOBA_TOP_K)
        sel_scr[hh] = jnp.where(sel, 1.0, 0.0)

    own = (i, [bias_ref[hh, 0] for hh in hs], None)

    def selected(b):
        return [sel_scr[hh, pl.ds(b, 1), :] for hh in hs]

    def far(b):
        return (b, [rb_ref[N_BUCKETS - 1, hp * heads + hh] for hh in hs], selected(b))

    n_far = jnp.maximum(i - 1, 0)
    attend([own, (n_far, [bias_ref[hh, 1] for hh in hs], selected(n_far))], True)

    def far_pair(j, carry):
        attend([far(2 * j), far(2 * j + 1)], False)
        return carry

    lax.fori_loop(0, lax.shift_right_logical(n_far, 1), far_pair, 0)

    @pl.when((n_far & 1) == 1)
    def _():
        attend([far(n_far - 1)], False)

    out_t = jnp.concatenate([acc_scr[hh] / l_scr[hh] for hh in range(heads)], axis=0)
    o_ref[...] = out_t.T


def _moba_prompt(q, k, v, bias_tiles, rel_bias, heads):
    nb, s, d = q.shape
    blk = MOBA_BLOCK
    n = HEAD_DIM
    gw = heads * n
    nblk = s // blk
    assert s % blk == 0 and gw % V7X_LANES == 0 and d % gw == 0
    qspec = pl.BlockSpec((None, blk, gw), lambda b, hp, i: (b, i, hp))
    kvspec = pl.BlockSpec((None, s, gw), lambda b, hp, i: (b, 0, hp))
    bspec = pl.BlockSpec((heads, 2, blk, blk), lambda b, hp, i: (hp, 0, 0, 0))
    return pl.pallas_call(
        _moba_prompt_body,
        grid=(nb, d // gw, nblk),
        in_specs=[pl.BlockSpec(memory_space=pltpu.SMEM), qspec, kvspec, kvspec, bspec],
        out_specs=qspec,
        out_shape=jax.ShapeDtypeStruct((nb, s, d), F32),
        scratch_shapes=[
            pltpu.VMEM((nblk, gw), F32),
            pltpu.VMEM((s, gw), BF16),
            pltpu.VMEM((nblk, heads, n + _ONES_ROWS, blk), BF16),
            pltpu.VMEM((heads, nblk, blk), F32),
            pltpu.VMEM((heads, 1, blk), F32),
            pltpu.VMEM((heads, 1, blk), F32),
            pltpu.VMEM((heads, n, blk), F32),
        ],
        compiler_params=_cparams(("arbitrary", "arbitrary", "arbitrary")),
        name="moba_prompt",
    )(rel_bias, q, k, v, bias_tiles)


_GATE_BLOCKS_PER_STEP = 8


def _moba_gate_body(pt_ref, qt_ref, *rest, ppb):
    top_ref, gate_scr = rest[-2:]
    page_refs = rest[:-2]
    step = pl.program_id(1)
    blocks_per_step = len(page_refs) // ppb
    nblocks = gate_scr.shape[0]
    rows = ppb * page_refs[0].shape[-1]
    for bi in range(blocks_per_step):
        ksum = page_refs[bi * ppb][...]
        for w in range(1, ppb):
            ksum = ksum + page_refs[bi * ppb + w][...]
        kmean = jnp.sum(ksum, axis=-1, keepdims=True) * (1.0 / rows)
        gate_scr[step * blocks_per_step + bi] = jnp.sum(qt_ref[...] * kmean, axis=1)

    @pl.when(step == pl.num_programs(1) - 1)
    def _():
        g = gate_scr[...]
        idx = lax.broadcasted_iota(jnp.int32, g.shape, 0)
        for j in range(MOBA_TOP_K):
            mx = jnp.max(g, axis=0, keepdims=True)
            first = jnp.min(jnp.where(g == mx, idx, nblocks), axis=0, keepdims=True)
            top_ref[j] = first[0]
            g = jnp.where(idx == first, -jnp.inf, g)


def _moba_gate_topk(qt, cache_kt, layer, page_table, n_past_blocks):
    nb, nh, n, t = qt.shape
    page = cache_kt.shape[-1]
    ppb = MOBA_BLOCK // page
    pages_per_step = _tile(n_past_blocks, _GATE_BLOCKS_PER_STEP) * ppb

    def kspec(which):
        return pl.BlockSpec((None, None, nh, n, page),
                            lambda b, j, pt: (layer, pt[b, j * pages_per_step + which], 0, 0, 0))

    grid_spec = pltpu.PrefetchScalarGridSpec(
        num_scalar_prefetch=1,
        grid=(nb, n_past_blocks * ppb // pages_per_step),
        in_specs=[pl.BlockSpec((None, nh, n, t), lambda b, j, pt: (b, 0, 0, 0))]
        + [kspec(w) for w in range(pages_per_step)],
        out_specs=pl.BlockSpec((None, MOBA_TOP_K, nh, t), lambda b, j, pt: (b, 0, 0, 0)),
        scratch_shapes=[pltpu.VMEM((n_past_blocks, nh, t), F32)],
    )
    return pl.pallas_call(
        functools.partial(_moba_gate_body, ppb=ppb),
        grid_spec=grid_spec,
        out_shape=jax.ShapeDtypeStruct((nb, MOBA_TOP_K, nh, t), jnp.int32),
        compiler_params=_cparams(("arbitrary", "arbitrary")),
        name="moba_gate",
    )(page_table, qt, *([cache_kt] * pages_per_step))


def _moba_sample_body(top_ref, pt_ref, rb_ref, q_ref, kn_ref, vn_ref, ck_ref, cv_ref, o_ref,
                      kbuf, vbuf, sems, *, layer, n_pages, past_len, nh):
    b = pl.program_id(0)
    h = pl.program_id(1)
    t, n = q_ref.shape
    page = kbuf.shape[-1]
    ppb = MOBA_BLOCK // page
    n_sel = MOBA_TOP_K * ppb
    scale = HEAD_DIM ** -0.5

    step = b * nh + h
    cur = lax.rem(step, 2)

    def block_of(ti, j, bb=b, hh=h):
        return top_ref[((bb * MOBA_TOP_K + j) * nh + hh) * t + ti]

    def copies(bb, hh, half, ti, j, p):
        phys = pt_ref[bb * n_pages + block_of(ti, j, bb, hh) * ppb + p]
        slot = (ti * MOBA_TOP_K + j) * ppb + p
        return (pltpu.make_async_copy(ck_ref.at[layer, phys, hh], kbuf.at[half, slot], sems.at[half, 0]),
                pltpu.make_async_copy(cv_ref.at[layer, phys, hh], vbuf.at[half, slot], sems.at[half, 1]))

    every = [(ti, j, p) for ti in range(t) for j in range(MOBA_TOP_K) for p in range(ppb)]

    def start_all(bb, hh, half):
        for idx in every:
            for cp in copies(bb, hh, half, *idx):
                cp.start()

    @pl.when(step == 0)
    def _():
        start_all(b, h, cur)

    @pl.when(step + 1 < pl.num_programs(0) * nh)
    def _():
        wraps = h == nh - 1
        start_all(jnp.where(wraps, b + 1, b), jnp.where(wraps, 0, h + 1), 1 - cur)

    for idx in every:
        for cp in copies(b, h, cur, *idx):
            cp.wait()

    qs = (q_ref[...] * scale).astype(BF16)
    n_slots = t * n_sel
    n_keys = n_slots * page
    seg = n_sel * page
    kt_all = jnp.concatenate([kbuf[cur, sl] for sl in range(n_slots)], axis=1)
    vt_all = jnp.concatenate([vbuf[cur, sl] for sl in range(n_slots)], axis=1)
    s_all = _mm(qs, kt_all)
    row = lax.broadcasted_iota(jnp.int32, (t, seg), 0)
    offs = lax.broadcasted_iota(jnp.int32, (1, MOBA_BLOCK), 1)
    s = jnp.zeros((t, seg), F32)
    kpos = jnp.zeros((t, seg), jnp.int32)
    for ti in range(t):
        s = jnp.where(row == ti, s_all[:, ti * seg:(ti + 1) * seg], s)
        pos = jnp.concatenate([block_of(ti, j) * MOBA_BLOCK + offs for j in range(MOBA_TOP_K)], axis=1)
        kpos = jnp.where(row == ti, pos, kpos)
    rel = (past_len + row) - kpos
    s = s + _bias_of_distance(jnp.maximum(rel, 0), rb_ref, h)
    r_o = lax.broadcasted_iota(jnp.int32, (t, t), 0)
    c_o = lax.broadcasted_iota(jnp.int32, (t, t), 1)
    rel_o = r_o - c_o
    s_own = _mm(qs, kn_ref[...], _NT) + _bias_of_distance(jnp.maximum(rel_o, 0), rb_ref, h)
    s_own = jnp.where(rel_o >= 0, s_own, NEG_BIG)
    m = jnp.maximum(jnp.max(s, axis=-1, keepdims=True), jnp.max(s_own, axis=-1, keepdims=True))
    p = jnp.exp(s - m)
    p_own = jnp.exp(s_own - m)
    denom = jnp.sum(p, axis=-1, keepdims=True) + jnp.sum(p_own, axis=-1, keepdims=True)
    p_all = jnp.concatenate([jnp.where(row == ti, p, 0.0) for ti in range(t)], axis=1)
    o_ref[...] = (_mm(p_all, vt_all, _NT) + _mm(p_own, vn_ref[...])) / denom


def _moba_sample(q, k_new, v_new, top, page_table, rel_bias, cache_kt, cache_vt, layer, past_len):
    nb, nh, t, n = q.shape
    page = cache_kt.shape[-1]
    n_pages = page_table.shape[1]
    ppb = MOBA_BLOCK // page
    tok = pl.BlockSpec((None, None, t, n), lambda b, h, *_: (b, h, 0, 0))
    any_spec = pl.BlockSpec(memory_space=pl.ANY)
    grid_spec = pltpu.PrefetchScalarGridSpec(
        num_scalar_prefetch=2,
        grid=(nb, nh),
        in_specs=[pl.BlockSpec(memory_space=pltpu.SMEM), tok, tok, tok, any_spec, any_spec],
        out_specs=tok,
        scratch_shapes=[
            pltpu.VMEM((2, t * MOBA_TOP_K * ppb, n, page), F32),
            pltpu.VMEM((2, t * MOBA_TOP_K * ppb, n, page), F32),
            pltpu.SemaphoreType.DMA((2, 2)),
        ],
    )
    return pl.pallas_call(
        functools.partial(_moba_sample_body, layer=layer, n_pages=n_pages, past_len=past_len, nh=nh),
        grid_spec=grid_spec,
        out_shape=jax.ShapeDtypeStruct((nb, nh, t, n), F32),
        compiler_params=_cparams(("arbitrary", "arbitrary")),
        name="moba_sample",
    )(top.reshape(-1), page_table.reshape(-1), rel_bias, q, k_new, v_new, cache_kt, cache_vt)


def _tile(m, pref):
    return pref if m % pref == 0 else m


def kernel(x_prompt, x_sample, state_wkv, state_shift, cache_k, cache_v, page_table, ln1, ln2, ln_f, mlp_up, mlp_down, rwkv_mix, rwkv_w_r, rwkv_w_k, rwkv_w_v, rwkv_w_o, rwkv_w0, rwkv_w1, rwkv_w2, rwkv_a0, rwkv_a1, rwkv_a2, rwkv_g1, rwkv_g2, rwkv_k_k, rwkv_k_a, rwkv_r_k, rwkv_ln_w, rwkv_ln_b, moba_w_qkv, moba_w_o, rel_bias):
    nb, s, d = x_prompt.shape
    nbd, t, _ = x_sample.shape
    n = HEAD_DIM
    nh = d // n
    depth = ln1.shape[0]
    assert depth == 2 and rwkv_mix.shape[0] == 1 and moba_w_qkv.shape[0] == 1
    page = cache_k.shape[2]
    past_len = page_table.shape[1] * page
    assert past_len % MOBA_BLOCK == 0 and MOBA_BLOCK % page == 0
    n_past_blocks = past_len // MOBA_BLOCK
    assert n_past_blocks >= MOBA_TOP_K and s % MOBA_BLOCK == 0

    row = lambda a: a.reshape(1, -1)
    bf = lambda a: a.astype(BF16)
    lnf = row(ln_f)

    proj_w = (row(ln1[0]), rwkv_mix[0], bf(rwkv_w_r[0]), bf(rwkv_w_k[0]), bf(rwkv_w_v[0]),
              bf(rwkv_w1[0]), bf(rwkv_a1[0]), bf(rwkv_g1[0]), bf(rwkv_w2[0]), bf(rwkv_a2[0]),
              bf(rwkv_g2[0]), row(rwkv_w0[0]), row(rwkv_a0[0]))
    scan_w = (row(rwkv_k_k[0]), row(rwkv_k_a[0]), row(rwkv_r_k[0]), row(rwkv_ln_w[0]), row(rwkv_ln_b[0]))
    mlp0 = (bf(rwkv_w_o[0]), row(ln2[0]), bf(mlp_up[0]), bf(mlp_down[0]), lnf)
    mlp1 = (bf(moba_w_o[0]), row(ln2[1]), bf(mlp_up[1]), bf(mlp_down[1]), lnf)

    def rwkv_layer(x, hprev, s0, tm, chunk, tm_mlp):
        streams, shift = _rwkv_proj(x, hprev, *proj_w, tm=tm)
        z, s_out = _rwkv_scan(*streams, s0, *scan_w, chunk=chunk,
                              chunks_per_step=_tile(x.shape[1] // chunk, 4), heads_per_step=_tile(nh, 16))
        y = _out_mlp(x.reshape(-1, d), z.reshape(-1, d), *mlp0, tm=tm_mlp, final_norm=False)
        return y, shift, s_out

    yp, shift_p, wkv_p = rwkv_layer(x_prompt, jnp.zeros((nb, d), F32), jnp.zeros((nb, nh, n, n), F32),
                                    _tile(s, 256), _tile(s, 64), _tile(nb * s, 512))
    ys, shift_s, wkv_s = rwkv_layer(x_sample, state_shift[0], state_wkv[0], t, t, nbd * t)

    w_qkv = bf(moba_w_qkv[0])
    qp, kp, vp, kp_t, vp_t = _qkv(yp.reshape(nb, s, d), row(ln1[1]), w_qkv, _tile(s, 512), True)
    qs, ks, vs = _qkv(ys.reshape(1, nbd * t, d), row(ln1[1]), w_qkv, nbd * t, False)
    bias_tiles = _moba_bias_tiles(rel_bias, MOBA_BLOCK)
    attn_p = _moba_prompt(qp, kp, vp, bias_tiles, rel_bias, heads=_tile(nh, 4))
    per_head = lambda a: a.reshape(nbd, t, nh, n)
    cache_kt = cache_k.transpose(0, 1, 3, 4, 2)
    cache_vt = cache_v.transpose(0, 1, 3, 4, 2)
    top = _moba_gate_topk(per_head(qs).transpose(0, 2, 3, 1), cache_kt, 0, page_table, n_past_blocks)
    heads_first = lambda a: per_head(a).transpose(0, 2, 1, 3)
    attn_s = _moba_sample(heads_first(qs), heads_first(ks), heads_first(vs), top, page_table,
                          rel_bias, cache_kt, cache_vt, 0, past_len)
    attn_s = attn_s.transpose(0, 2, 1, 3).reshape(nbd * t, d)
    y_prompt = _out_mlp(yp, attn_p.reshape(-1, d), *mlp1, tm=_tile(nb * s, 512), final_norm=True)
    y_sample = _out_mlp(ys, attn_s, *mlp1, tm=nbd * t, final_norm=True)

    seq_last = lambda a: a.reshape(1, nb, nh, n, s).transpose(0, 1, 4, 2, 3)
    return (y_prompt.reshape(nb, s, d), y_sample.reshape(nbd, t, d),
            wkv_p[None], shift_p[None], seq_last(kp_t), seq_last(vp_t),
            wkv_s[None], shift_s[None],
            ks.reshape(1, nbd, t, nh, n), vs.reshape(1, nbd, t, nh, n))
```
